```python
import jax, jax.numpy as jnp
from jax import lax
import numpy as np

D_MODEL = 1024
BATCH = 16
SEQ = 2048
DEPTH = 1
DEC_BATCH = 32
DEC_SEQ = 64
PAST_LEN = 4096

CHUNK = 64
GDN_HEADS = 8
GDN_DK = 128
GDN_DV = 128
GDN_QK_DIM = GDN_HEADS * GDN_DK
GDN_V_DIM = GDN_HEADS * GDN_DV
GDN_CONV_DIM = 2 * GDN_QK_DIM + GDN_V_DIM
CONV_W = 4
SWA_HQ = 16
SWA_HKV = 4
SWA_HD = 64
SWA_GROUP = SWA_HQ // SWA_HKV
WINDOW = 128
WIN_CHUNKS = WINDOW // CHUNK
D_FF = 2816
N_MOD = 9
EPS = 1e-6
IN_SIZES = (GDN_CONV_DIM, GDN_V_DIM, GDN_HEADS, GDN_HEADS, SWA_HQ * SWA_HD, SWA_HKV * SWA_HD, SWA_HKV * SWA_HD, 2 * D_MODEL)
IN_DIM = GDN_CONV_DIM + GDN_V_DIM + 2 * GDN_HEADS + SWA_HQ * SWA_HD + 2 * SWA_HKV * SWA_HD + 2 * D_MODEL

kernel_name = 'streaming_gdn_swa_macaron_adaln'


def rms_norm(x, gain):
    xf = x.astype(jnp.float32)
    y = xf * lax.rsqrt(jnp.mean(xf * xf, axis=-1, keepdims=True) + EPS)
    return (y * gain.astype(jnp.float32)).astype(x.dtype)


def l2_norm(x):
    xf = x.astype(jnp.float32)
    return xf * lax.rsqrt(jnp.sum(xf * xf, axis=-1, keepdims=True) + EPS)


def modulate(x, gain, shift, scale):
    return rms_norm(x, gain) * (1 + scale) + shift


def swiglu(h, w_in, w_out):
    gate, up = jnp.split(h @ w_in, 2, axis=-1)
    return (jax.nn.silu(gate) * up) @ w_out


def causal_conv(x, prefix, w):
    T = x.shape[1]
    xp = jnp.concatenate([prefix.astype(x.dtype), x], axis=1)
    y = xp[:, 0:T] * w[0]
    for i in range(1, CONV_W):
        y = y + xp[:, i:i + T] * w[i]
    return jax.nn.silu(y), xp[:, -(CONV_W - 1):]


def gated_delta_chunked(q, k, v, g, beta, S0):
    B, T, H, DK = q.shape
    DV = v.shape[-1]
    L = min(CHUNK, T)
    NC = T // L

    def blk(a):
        a = a.reshape((B, NC, L, H) + a.shape[3:])
        return jnp.moveaxis(a, 3, 1)

    q, k, v, g, beta = blk(q), blk(k), blk(v), blk(g), blk(beta)
    G = jnp.cumsum(g, axis=-1)
    incl = jnp.tril(jnp.ones((L, L), bool))
    strict = jnp.tril(jnp.ones((L, L), bool), -1)
    gamma = jnp.exp(jnp.where(incl, G[..., :, None] - G[..., None, :], -jnp.inf))
    kk = jnp.einsum('bhcid,bhcjd->bhcij', k, k)
    A = jnp.eye(L, dtype=jnp.float32) + jnp.where(strict, beta[..., :, None] * kk * gamma, 0.0)
    rhs = jnp.concatenate([v * beta[..., None], k * (beta * jnp.exp(G))[..., None]], axis=-1)
    X = lax.linalg.triangular_solve(A, rhs, left_side=True, lower=True, unit_diagonal=True)
    u, w = X[..., :DV], X[..., DV:]
    qk = jnp.einsum('bhcid,bhcjd->bhcij', q, k) * gamma
    q_dec = q * jnp.exp(G)[..., None]
    k_dec = k * jnp.exp(G[..., -1:] - G)[..., None]
    d_last = jnp.exp(G[..., -1])

    def step(S, xs):
        u_c, w_c, qk_c, qd_c, kd_c, dl_c = xs
        v_new = u_c - jnp.einsum('bhid,bhde->bhie', w_c, S)
        o = jnp.einsum('bhid,bhde->bhie', qd_c, S) + jnp.einsum('bhij,bhje->bhie', qk_c, v_new)
        S = dl_c[..., None, None] * S + jnp.einsum('bhid,bhie->bhde', kd_c, v_new)
        return S, o

    xs = tuple(jnp.moveaxis(a, 2, 0) for a in (u, w, qk, q_dec, k_dec, d_last))
    S, o = lax.scan(step, S0, xs)
    o = jnp.moveaxis(o, 0, 2).reshape(B, H, T, DV)
    return jnp.transpose(o, (0, 2, 1, 3)), S


def sink_attention(q, k, v, mask, sinks):
    s = jnp.einsum('bclkgd,bcskd->bckgls', q, k, preferred_element_type=jnp.float32) * (SWA_HD ** -0.5)
    s = jnp.where(mask[None, :, None, None], s, -jnp.inf)
    sink = sinks.astype(jnp.float32).reshape(SWA_HKV, SWA_GROUP)[None, None, :, :, None, None]
    m = jnp.maximum(jnp.max(s, axis=-1, keepdims=True), sink)
    p = jnp.exp(s - m)
    p = p / (jnp.sum(p, axis=-1, keepdims=True) + jnp.exp(sink - m))
    return jnp.einsum('bckgls,bcskd->bclkgd', p.astype(v.dtype), v)


def swa_prompt(q, k, v, sinks):
    B, T = q.shape[:2]
    NC = T // CHUNK
    qb = q.reshape(B, NC, CHUNK, SWA_HKV, SWA_GROUP, SWA_HD)

    def band(a):
        pad = jnp.zeros((B, WINDOW) + a.shape[2:], a.dtype)
        ac = jnp.concatenate([pad, a], axis=1).reshape((B, NC + WIN_CHUNKS, CHUNK) + a.shape[2:])
        return jnp.concatenate([ac[:, j:j + NC] for j in range(WIN_CHUNKS + 1)], axis=2)

    key_chunk = jnp.arange(NC)[:, None] - WIN_CHUNKS + jnp.arange(WIN_CHUNKS + 1)[None, :]
    valid = jnp.repeat(key_chunk >= 0, CHUNK, axis=1)
    o = sink_attention(qb, band(k), band(v), valid[:, None, :], sinks)
    return o.reshape(B, T, SWA_HQ * SWA_HD)


def swa_sample(q, k, v, k_cache, v_cache, sinks):
    B, T = q.shape[:2]
    kf = jnp.concatenate([k_cache.astype(k.dtype), k], axis=1)
    vf = jnp.concatenate([v_cache.astype(v.dtype), v], axis=1)
    qb = q.reshape(B, 1, T, SWA_HKV, SWA_GROUP, SWA_HD)
    mask = jnp.ones((1, 1, WINDOW + T), bool)
    o = sink_attention(qb, kf[:, None], vf[:, None], mask, sinks)
    return o.reshape(B, T, SWA_HQ * SWA_HD), kf[:, -WINDOW:], vf[:, -WINDOW:]


def trunk_layer(x, c, conv_prefix, S0, k_cache, v_cache, lp, is_prompt):
    (w_ada, b_ada, norm_ffn1, ffn1_w_in, ffn1_w_out, norm_mix, w_in, gdn_conv_w,
     gdn_a_log, gdn_dt_bias, gdn_norm, swa_q_norm, swa_k_norm, swa_sinks, b_merge,
     w_out, norm_ffn2, ffn2_w_in, ffn2_w_out) = lp
    B, T, _ = x.shape
    mod = (jax.nn.silu(c) @ w_ada + b_ada)[:, None, :]
    sh1, sc1, gt1, sh2, sc2, gt2, sh3, sc3, gt3 = jnp.split(mod, N_MOD, axis=-1)

    x = x + 0.5 * gt1 * swiglu(modulate(x, norm_ffn1, sh1, sc1), ffn1_w_in, ffn1_w_out)

    h = modulate(x, norm_mix, sh2, sc2)
    offsets = [int(o) for o in np.cumsum(IN_SIZES)[:-1]]
    conv_in, z, a, b, q_s, k_s, v_s, gate_logits = jnp.split(h @ w_in, offsets, axis=-1)

    conv_out, new_conv = causal_conv(conv_in, conv_prefix, gdn_conv_w)
    qg, kg, vg = jnp.split(conv_out, [GDN_QK_DIM, 2 * GDN_QK_DIM], axis=-1)
    qg = l2_norm(qg.reshape(B, T, GDN_HEADS, GDN_DK)) * (GDN_DK ** -0.5)
    kg = l2_norm(kg.reshape(B, T, GDN_HEADS, GDN_DK))
    vg = vg.reshape(B, T, GDN_HEADS, GDN_DV).astype(jnp.float32)
    g = -jnp.exp(gdn_a_log.astype(jnp.float32)) * jax.nn.softplus(a.astype(jnp.float32) + gdn_dt_bias.astype(jnp.float32))
    beta = jax.nn.sigmoid(b.astype(jnp.float32))
    o_g, new_S = gated_delta_chunked(qg, kg, vg, g, beta, S0.astype(jnp.float32))
    o_g = rms_norm(o_g, gdn_norm).astype(x.dtype).reshape(B, T, GDN_V_DIM) * jax.nn.silu(z)

    q_s = rms_norm(q_s.reshape(B, T, SWA_HQ, SWA_HD), swa_q_norm)
    k_s = rms_norm(k_s.reshape(B, T, SWA_HKV, SWA_HD), swa_k_norm)
    v_s = v_s.reshape(B, T, SWA_HKV, SWA_HD)
    if is_prompt:
        o_s = swa_prompt(q_s, k_s, v_s, swa_sinks)
        new_k, new_v = k_s[:, -WINDOW:], v_s[:, -WINDOW:]
    else:
        o_s, new_k, new_v = swa_sample(q_s, k_s, v_s, k_cache, v_cache, swa_sinks)

    g_a, g_b = jnp.split(jax.nn.sigmoid(gate_logits + b_merge), 2, axis=-1)
    x = x + gt2 * ((g_a * o_g + g_b * o_s) @ w_out)

    x = x + 0.5 * gt3 * swiglu(modulate(x, norm_ffn2, sh3, sc3), ffn2_w_in, ffn2_w_out)
    return x, new_conv, new_S, new_k, new_v


def setup_inputs(seed: int = 0) -> dict:
    key = jax.random.key(seed)
    ks = jax.random.split(key, 32)

    def nrm(k, shape, s):
        return jax.random.normal(k, shape, jnp.float32) * s

    dt = jnp.exp(jax.random.uniform(ks[17], (DEPTH, GDN_HEADS), jnp.float32, np.log(1e-3), np.log(1e-1)))
    return {
        'x_prompt': nrm(ks[0], (BATCH, SEQ, D_MODEL), 1.0),
        'x_sample': nrm(ks[1], (DEC_BATCH, DEC_SEQ, D_MODEL), 1.0),
        'state_gdn_conv': nrm(ks[2], (DEPTH, DEC_BATCH, CONV_W - 1, GDN_CONV_DIM), 1.0),
        'state_gdn': nrm(ks[3], (DEPTH, DEC_BATCH, GDN_HEADS, GDN_DK, GDN_DV), 0.1),
        'cache_swa_k': nrm(ks[4], (DEPTH, DEC_BATCH, WINDOW, SWA_HKV, SWA_HD), 1.0),
        'cache_swa_v': nrm(ks[5], (DEPTH, DEC_BATCH, WINDOW, SWA_HKV, SWA_HD), 1.0),
        'c_prompt': nrm(ks[6], (BATCH, D_MODEL), 1.0),
        'c_sample': nrm(ks[7], (DEC_BATCH, D_MODEL), 1.0),
        'w_ada': nrm(ks[8], (DEPTH, D_MODEL, N_MOD * D_MODEL), 0.3 * D_MODEL ** -0.5),
        'b_ada': nrm(ks[9], (DEPTH, N_MOD * D_MODEL), 0.01),
        'norm_ffn1': 1.0 + nrm(ks[10], (DEPTH, D_MODEL), 0.01),
        'ffn1_w_in': nrm(ks[11], (DEPTH, D_MODEL, 2 * D_FF), D_MODEL ** -0.5),
        'ffn1_w_out': nrm(ks[12], (DEPTH, D_FF, D_MODEL), D_FF ** -0.5),
        'norm_mix': 1.0 + nrm(ks[13], (DEPTH, D_MODEL), 0.01),
        'w_in': nrm(ks[14], (DEPTH, D_MODEL, IN_DIM), D_MODEL ** -0.5),
        'gdn_conv_w': nrm(ks[15], (DEPTH, CONV_W, GDN_CONV_DIM), CONV_W ** -0.5),
        'gdn_a_log': jnp.log(jax.random.uniform(ks[16], (DEPTH, GDN_HEADS), jnp.float32, 1.0, 16.0)),
        'gdn_dt_bias': dt + jnp.log(-jnp.expm1(-dt)),
        'gdn_norm': 1.0 + nrm(ks[18], (DEPTH, GDN_DV), 0.01),
        'swa_q_norm': 1.0 + nrm(ks[19], (DEPTH, SWA_HD), 0.01),
        'swa_k_norm': 1.0 + nrm(ks[20], (DEPTH, SWA_HD), 0.01),
        'swa_sinks': nrm(ks[21], (DEPTH, SWA_HQ), 0.5),
        'b_merge': nrm(ks[22], (DEPTH, 2 * D_MODEL), 0.01),
        'w_out': nrm(ks[23], (DEPTH, D_MODEL, D_MODEL), D_MODEL ** -0.5),
        'norm_ffn2': 1.0 + nrm(ks[24], (DEPTH, D_MODEL), 0.01),
        'ffn2_w_in': nrm(ks[25], (DEPTH, D_MODEL, 2 * D_FF), D_MODEL ** -0.5),
        'ffn2_w_out': nrm(ks[26], (DEPTH, D_FF, D_MODEL), D_FF ** -0.5),
    }


def reference(x_prompt, x_sample, state_gdn_conv, state_gdn, cache_swa_k, cache_swa_v, c_prompt, c_sample,
              w_ada, b_ada, norm_ffn1, ffn1_w_in, ffn1_w_out, norm_mix, w_in, gdn_conv_w, gdn_a_log,
              gdn_dt_bias, gdn_norm, swa_q_norm, swa_k_norm, swa_sinks, b_merge, w_out, norm_ffn2,
              ffn2_w_in, ffn2_w_out):
    yp, ys = x_prompt, x_sample
    conv_p, gdn_p, k_p, v_p = [], [], [], []
    conv_s, gdn_s, k_s, v_s = [], [], [], []
    for l in range(DEPTH):
        lp = (w_ada[l], b_ada[l], norm_ffn1[l], ffn1_w_in[l], ffn1_w_out[l], norm_mix[l], w_in[l],
              gdn_conv_w[l], gdn_a_log[l], gdn_dt_bias[l], gdn_norm[l], swa_q_norm[l], swa_k_norm[l],
              swa_sinks[l], b_merge[l], w_out[l], norm_ffn2[l], ffn2_w_in[l], ffn2_w_out[l])
        zero_conv = jnp.zeros((yp.shape[0], CONV_W - 1, GDN_CONV_DIM), yp.dtype)
        zero_S = jnp.zeros((yp.shape[0], GDN_HEADS, GDN_DK, GDN_DV), jnp.float32)
        yp, cp, sp, kp, vp = trunk_layer(yp, c_prompt, zero_conv, zero_S, None, None, lp, True)
        ys, cs, ss, kss, vss = trunk_layer(ys, c_sample, state_gdn_conv[l], state_gdn[l],
                                           cache_swa_k[l], cache_swa_v[l], lp, False)
        conv_p.append(cp); gdn_p.append(sp); k_p.append(kp); v_p.append(vp)
        conv_s.append(cs); gdn_s.append(ss); k_s.append(kss); v_s.append(vss)
    return (yp, ys,
            jnp.stack(conv_p), jnp.stack(gdn_p), jnp.stack(k_p), jnp.stack(v_p),
            jnp.stack(conv_s), jnp.stack(gdn_s), jnp.stack(k_s), jnp.stack(v_s))
```

```python
import functools

import jax
import jax.numpy as jnp
from jax import lax
from jax.experimental import pallas as pl
from jax.experimental.pallas import tpu as pltpu

F32 = jnp.float32
BF16 = jnp.bfloat16

CHUNK = 64
GDN_HEADS = 8
GDN_DK = 128
GDN_DV = 128
GDN_QK_DIM = GDN_HEADS * GDN_DK
GDN_V_DIM = GDN_HEADS * GDN_DV
GDN_CONV_DIM = 2 * GDN_QK_DIM + GDN_V_DIM
CONV_W = 4
SWA_HQ = 16
SWA_HKV = 4
SWA_HD = 64
SWA_GROUP = SWA_HQ // SWA_HKV
SWA_Q_DIM = SWA_HQ * SWA_HD
SWA_KV_DIM = SWA_HKV * SWA_HD
WINDOW = 128
N_MOD = 9
EPS = 1e-6
LANES = 128
VMEM_LIMIT_BYTES = 60 * 1024 * 1024


def _dot(a, b):
    return jnp.dot(a, b, preferred_element_type=F32)


def _dot_nt(a, b):
    return lax.dot_general(a, b, (((1,), (1,)), ((), ())), preferred_element_type=F32)


def _dot_tn(a, b, precision=None):
    return lax.dot_general(a, b, (((0,), (0,)), ((), ())), preferred_element_type=F32,
                           precision=precision)


def _silu(x):
    return x * jax.nn.sigmoid(x)


def _softplus(x):
    return jnp.maximum(x, 0.0) + jnp.log1p(jnp.exp(-jnp.abs(x)))


def _modulate(x, gain, shift, scale):
    y = x * lax.rsqrt(jnp.mean(x * x, axis=-1, keepdims=True) + EPS)
    return (y * gain) * (1 + scale) + shift


def _const_spec(shape):
    nd = len(shape)
    return pl.BlockSpec(shape, lambda *_: (0,) * nd, pipeline_mode=pl.Buffered(1))


def _params(semantics):
    return pltpu.CompilerParams(dimension_semantics=semantics, vmem_limit_bytes=VMEM_LIMIT_BYTES)


def _adaln_kernel(c_ref, w_ref, b_ref, o_ref):
    s = _silu(c_ref[...]).astype(BF16)
    o_ref[0] = _dot(s, w_ref[...].astype(BF16)) + b_ref[0]


def _adaln(c, w_ada, b_ada):
    nb, d = c.shape
    return pl.pallas_call(
        _adaln_kernel,
        out_shape=jax.ShapeDtypeStruct((N_MOD, nb, d), F32),
        grid=(N_MOD,),
        in_specs=[pl.BlockSpec((nb, d), lambda j: (0, 0)),
                  pl.BlockSpec((d, d), lambda j: (0, j)),
                  pl.BlockSpec((1, 1, d), lambda j: (j, 0, 0))],
        out_specs=pl.BlockSpec((1, nb, d), lambda j: (j, 0, 0)),
        compiler_params=_params(("parallel",)),
        name="adaln",
    )(c, w_ada, b_ada.reshape(N_MOD, 1, d))


def _ffn_body(x, mod_ref, gain_ref, win_ref, wout_ref, act_ref, tf):
    bb, tt, d = x.shape
    d_ff = wout_ref.shape[0]
    sh, sc, gt = mod_ref[0], mod_ref[1], mod_ref[2]
    h = _modulate(x, gain_ref[...], sh, sc).reshape(bb * tt, d).astype(BF16)
    for j in range(d_ff // tf):
        gu = _dot(h, win_ref[:, 2 * j * tf:2 * (j + 1) * tf])
        act_ref[:, j * tf:(j + 1) * tf] = (_silu(gu[:, :tf]) * gu[:, tf:]).astype(BF16)
    y = _dot(act_ref[...], wout_ref[...])
    return x + (0.5 * gt) * y.reshape(bb, tt, d)


def _ffn_kernel(x_ref, mod_ref, gain_ref, win_ref, wout_ref, o_ref, act_ref, *, tf):
    o_ref[...] = _ffn_body(x_ref[...], mod_ref, gain_ref, win_ref, wout_ref, act_ref, tf)


def _mix_ffn_kernel(x_ref, og_ref, os_ref, gates_ref, bm_ref, wo_ref, gt2_ref,
                    mod_ref, gain_ref, win_ref, wout_ref, o_ref, act_ref, *, tf):
    x = x_ref[...]
    bb, tt, d = x.shape
    gl = jax.nn.sigmoid(gates_ref[...] + bm_ref[...])
    mix = gl[..., :d] * og_ref[...] + gl[..., d:] * os_ref[...]
    y = _dot(mix.reshape(bb * tt, d).astype(BF16), wo_ref[...]).reshape(bb, tt, d)
    x1 = x + gt2_ref[0] * y
    o_ref[...] = _ffn_body(x1, mod_ref, gain_ref, win_ref, wout_ref, act_ref, tf)


def _tok_spec(bb, tt, width):
    return pl.BlockSpec((bb, tt, width), lambda b, t: (b, t, 0))


def _mod_spec(n, bb, d):
    return pl.BlockSpec((n, bb, 1, d), lambda b, t: (0, b, 0, 0))


def _ffn(x, mod3, gain, w_in_packed, w_out, *, bb, tt, tf):
    B, T, d = x.shape
    d_ff = w_out.shape[0]
    return pl.pallas_call(
        functools.partial(_ffn_kernel, tf=tf),
        out_shape=jax.ShapeDtypeStruct(x.shape, F32),
        grid=(B // bb, T // tt),
        in_specs=[_tok_spec(bb, tt, d), _mod_spec(3, bb, d), _const_spec((1, d)),
                  _const_spec((d, 2 * d_ff)), _const_spec((d_ff, d))],
        out_specs=_tok_spec(bb, tt, d),
        scratch_shapes=[pltpu.VMEM((bb * tt, d_ff), BF16)],
        compiler_params=_params(("parallel", "parallel")),
        name="ffn",
    )(x, mod3, gain, w_in_packed, w_out)


def _mix_ffn(x, og, os_, gates, b_merge, w_o, mod_mix, mod3, gain, w_in_packed, w_out, *, bb, tt, tf):
    B, T, d = x.shape
    d_ff = w_out.shape[0]
    return pl.pallas_call(
        functools.partial(_mix_ffn_kernel, tf=tf),
        out_shape=jax.ShapeDtypeStruct(x.shape, F32),
        grid=(B // bb, T // tt),
        in_specs=[_tok_spec(bb, tt, d), _tok_spec(bb, tt, d), _tok_spec(bb, tt, d),
                  _tok_spec(bb, tt, 2 * d), _const_spec((1, 2 * d)), _const_spec((d, d)),
                  pl.BlockSpec((1, bb, 1, d), lambda b, t: (2, b, 0, 0)),
                  _mod_spec(3, bb, d), _const_spec((1, d)),
                  _const_spec((d, 2 * d_ff)), _const_spec((d_ff, d))],
        out_specs=_tok_spec(bb, tt, d),
        scratch_shapes=[pltpu.VMEM((bb * tt, d_ff), BF16)],
        compiler_params=_params(("parallel", "parallel")),
        name="mix_ffn",
    )(x, og, os_, gates, b_merge, w_o, mod_mix, mod3, gain, w_in_packed, w_out)


_PROJ_STEP = 512
_AB_WIDTH = 2 * LANES


def _inproj_kernel(x_ref, mod_ref, gain_ref, w_ref, wab_ref,
                   conv_ref, z_ref, q_ref, kv_ref, gates_ref, ab_ref):
    x = x_ref[...]
    bb, tt, d = x.shape
    h = _modulate(x, gain_ref[...], mod_ref[0], mod_ref[1]).reshape(bb * tt, d).astype(BF16)
    col = 0
    for ref in (conv_ref, z_ref, q_ref, kv_ref, gates_ref):
        width = ref.shape[-1]
        for lo in range(0, width, _PROJ_STEP):
            hi = min(lo + _PROJ_STEP, width)
            ref[:, :, lo:hi] = _dot(h, w_ref[:, col + lo:col + hi]).reshape(bb, tt, hi - lo)
        col += width
    ab_ref[...] = _dot(h, wab_ref[...]).reshape(bb, tt, _AB_WIDTH)


def _inproj(x, mod3, gain, w_main, w_ab, *, bb, tt):
    B, T, d = x.shape
    widths = (GDN_CONV_DIM, GDN_V_DIM, SWA_Q_DIM, 2 * SWA_KV_DIM, 2 * d, _AB_WIDTH)
    return pl.pallas_call(
        _inproj_kernel,
        out_shape=[jax.ShapeDtypeStruct((B, T, w), F32) for w in widths],
        grid=(B // bb, T // tt),
        in_specs=[_tok_spec(bb, tt, d), _mod_spec(3, bb, d), _const_spec((1, d)),
                  _const_spec(w_main.shape), _const_spec(w_ab.shape)],
        out_specs=[_tok_spec(bb, tt, w) for w in widths],
        compiler_params=_params(("parallel", "parallel")),
        name="inproj",
    )(x, mod3, gain, w_main, w_ab)


_XP_PAD = 8


def _merge_masks(size):
    row = lax.broadcasted_iota(jnp.int32, (size, size), 0)
    col = lax.broadcasted_iota(jnp.int32, (size, size), 1)
    masks = []
    s = 1
    while s < size:
        rb, cb = row // s, col // s
        masks.append((rb == cb + 1) & (rb % 2 == 1))
        s *= 2
    return row == col, masks


def _unit_lower_inverse(n_mat, eye, masks):
    t = jnp.where(eye, 1.0, 0.0) - jnp.where(masks[0], n_mat, 0.0)
    for mask in masks[1:]:
        c = jnp.where(mask, n_mat, 0.0).astype(BF16)
        tb = t.astype(BF16)
        t = t - _dot(tb, _dot(c, tb).astype(BF16))
    return t


def _gdn_kernel(conv_ref, prefix_ref, ab_ref, z_ref, s0_ref, convw_ref, alog_ref, dtb_ref, gnorm_ref,
                o_ref, sout_ref, convout_ref, s_scr, xp_scr):
    c = pl.program_id(1)
    L = CHUNK

    @pl.when(c == 0)
    def _():
        s_scr[...] = s0_ref[0]
        xp_scr[_XP_PAD - (CONV_W - 1):_XP_PAD, :] = prefix_ref[0]

    x = conv_ref[0]
    xp_scr[_XP_PAD:_XP_PAD + L, :] = x
    w = convw_ref[...]
    y = x * w[CONV_W - 1:CONV_W]
    for i in range(CONV_W - 1):
        lo = _XP_PAD - (CONV_W - 1) + i
        y = y + xp_scr[lo:lo + L, :] * w[i:i + 1]
    y = _silu(y)
    tail = x[L - _XP_PAD:, :]
    xp_scr[0:_XP_PAD, :] = tail

    ab = ab_ref[0]
    g = -jnp.exp(alog_ref[...]) * _softplus(ab[:, :LANES] + dtb_ref[...])
    beta = jax.nn.sigmoid(ab[:, LANES:])
    row = lax.broadcasted_iota(jnp.int32, (L, L), 0)
    col = lax.broadcasted_iota(jnp.int32, (L, L), 1)
    incl = row >= col
    strict = row > col
    gc_all = jnp.dot(incl.astype(F32), g, preferred_element_type=F32,
                     precision=lax.Precision.HIGHEST)
    gr_all = _dot_tn(g, (col >= row).astype(F32), precision=lax.Precision.HIGHEST)
    glast_all = gc_all[L - 1:L, :]
    eye, masks = _merge_masks(L)
    gnorm = gnorm_ref[...]
    z = z_ref[0]

    for h in range(GDN_HEADS):
        qh = y[:, h * GDN_DK:(h + 1) * GDN_DK]
        kh = y[:, GDN_QK_DIM + h * GDN_DK:GDN_QK_DIM + (h + 1) * GDN_DK]
        vh = y[:, 2 * GDN_QK_DIM + h * GDN_DV:2 * GDN_QK_DIM + (h + 1) * GDN_DV]
        qn = qh * lax.rsqrt(jnp.sum(qh * qh, axis=-1, keepdims=True) + EPS) * (GDN_DK ** -0.5)
        kn = kh * lax.rsqrt(jnp.sum(kh * kh, axis=-1, keepdims=True) + EPS)
        gc = gc_all[:, h:h + 1]
        gr = gr_all[h:h + 1, :]
        gl = glast_all[:, h:h + 1]
        bt = beta[:, h:h + 1]
        gamma = jnp.exp(jnp.where(incl, gc - gr, -jnp.inf))
        eg = jnp.exp(gc)
        qb = qn.astype(BF16)
        kb = kn.astype(BF16)
        kk = _dot_nt(kb, kb)
        n_mat = jnp.where(strict, bt * kk * gamma, 0.0)
        rhs = jnp.concatenate([vh * bt, kn * (bt * eg)], axis=-1)
        t_inv = _unit_lower_inverse(n_mat, eye, masks)
        uw = _dot(t_inv.astype(BF16), rhs.astype(BF16))
        u, wmat = uw[:, :GDN_DV], uw[:, GDN_DV:]
        qk = _dot_nt(qb, kb) * gamma
        q_dec = (qn * eg).astype(BF16)
        k_dec = (kn * jnp.exp(gl - gc)).astype(BF16)
        s_old = s_scr[h]
        s_b = s_old.astype(BF16)
        v_new = u - _dot(wmat.astype(BF16), s_b)
        v_new_b = v_new.astype(BF16)
        o = _dot(q_dec, s_b) + _dot(qk.astype(BF16), v_new_b)
        s_scr[h] = jnp.exp(gl) * s_old + _dot_tn(k_dec, v_new_b)
        on = o * lax.rsqrt(jnp.mean(o * o, axis=-1, keepdims=True) + EPS) * gnorm
        o_ref[0, :, h * GDN_DV:(h + 1) * GDN_DV] = on * _silu(z[:, h * GDN_DV:(h + 1) * GDN_DV])

    @pl.when(c == pl.num_programs(1) - 1)
    def _():
        sout_ref[0] = s_scr[...]
        convout_ref[0] = tail[_XP_PAD - (CONV_W - 1):, :]


def _gdn(conv_in, prefix, ab, z, s0, conv_w, alog_pad, dtb_pad, gnorm):
    B, T, cd = conv_in.shape
    nc = T // CHUNK
    return pl.pallas_call(
        _gdn_kernel,
        out_shape=[jax.ShapeDtypeStruct((B, T, GDN_V_DIM), F32),
                   jax.ShapeDtypeStruct(s0.shape, F32),
                   jax.ShapeDtypeStruct((B, CONV_W - 1, cd), F32)],
        grid=(B, nc),
        in_specs=[pl.BlockSpec((1, CHUNK, cd), lambda b, c: (b, c, 0)),
                  pl.BlockSpec((1, CONV_W - 1, cd), lambda b, c: (b, 0, 0)),
                  pl.BlockSpec((1, CHUNK, _AB_WIDTH), lambda b, c: (b, c, 0)),
                  pl.BlockSpec((1, CHUNK, GDN_V_DIM), lambda b, c: (b, c, 0)),
                  pl.BlockSpec((1,) + s0.shape[1:], lambda b, c: (b, 0, 0, 0)),
                  _const_spec(conv_w.shape), _const_spec((1, LANES)), _const_spec((1, LANES)),
                  _const_spec((1, GDN_DV))],
        out_specs=[pl.BlockSpec((1, CHUNK, GDN_V_DIM), lambda b, c: (b, c, 0)),
                   pl.BlockSpec((1,) + s0.shape[1:], lambda b, c: (b, 0, 0, 0)),
                   pl.BlockSpec((1, CONV_W - 1, cd), lambda b, c: (b, 0, 0))],
        scratch_shapes=[pltpu.VMEM(s0.shape[1:], F32),
                        pltpu.VMEM((_XP_PAD + CHUNK, cd), F32)],
        compiler_params=_params(("parallel", "arbitrary")),
        name="gdn",
    )(conv_in, prefix, ab, z, s0, conv_w, alog_pad, dtb_pad, gnorm)


def _head_rms(x, gain):
    return x * lax.rsqrt(jnp.mean(x * x, axis=-1, keepdims=True) + EPS) * gain


def _swa_kernel(q_ref, kv_ref, kc_ref, vc_ref, qg_ref, kg_ref, sink_ref,
                o_ref, kout_ref, vout_ref, kwin, vwin, *, tq, mask_history):
    c = pl.program_id(1)
    L = CHUNK
    span = WINDOW + L

    @pl.when(c == 0)
    def _():
        kwin[0:WINDOW, :] = kc_ref[0]
        vwin[0:WINDOW, :] = vc_ref[0]

    kv = kv_ref[0]
    kg = kg_ref[...]
    for hk in range(SWA_HKV):
        kwin[WINDOW:WINDOW + tq, hk * SWA_HD:(hk + 1) * SWA_HD] = _head_rms(
            kv[:, hk * SWA_HD:(hk + 1) * SWA_HD], kg)
    vwin[WINDOW:WINDOW + tq, :] = kv[:, SWA_KV_DIM:]

    qg = qg_ref[...]
    scale = SWA_HD ** -0.5
    for j in range(tq // L):
        if mask_history:
            kpos = c * tq + (j * L - WINDOW) + lax.broadcasted_iota(jnp.int32, (L, span), 1)
            valid = kpos >= 0
        for hk in range(SWA_HKV):
            k_h = kwin[j * L:j * L + span, hk * SWA_HD:(hk + 1) * SWA_HD].astype(BF16)
            v_h = vwin[j * L:j * L + span, hk * SWA_HD:(hk + 1) * SWA_HD].astype(BF16)
            outs = []
            for gq in range(SWA_GROUP):
                h = hk * SWA_GROUP + gq
                q_h = _head_rms(q_ref[0, j * L:(j + 1) * L, h * SWA_HD:(h + 1) * SWA_HD], qg) * scale
                s = _dot_nt(q_h.astype(BF16), k_h)
                if mask_history:
                    s = jnp.where(valid, s, -jnp.inf)
                sink = sink_ref[:, h:h + 1]
                m = jnp.maximum(jnp.max(s, axis=-1, keepdims=True), sink)
                p = jnp.exp(s - m)
                den = jnp.sum(p, axis=-1, keepdims=True) + jnp.exp(sink - m)
                outs.append(_dot(p.astype(BF16), v_h) / den)
            for pair in range(SWA_GROUP // 2):
                lo = (hk * SWA_GROUP + 2 * pair) * SWA_HD
                o_ref[0, j * L:(j + 1) * L, lo:lo + 2 * SWA_HD] = jnp.concatenate(
                    outs[2 * pair:2 * pair + 2], axis=-1)

    knew = kwin[tq:tq + WINDOW, :]
    vnew = vwin[tq:tq + WINDOW, :]
    kwin[0:WINDOW, :] = knew
    vwin[0:WINDOW, :] = vnew

    @pl.when(c == pl.num_programs(1) - 1)
    def _():
        kout_ref[0] = knew
        vout_ref[0] = vnew


def _swa(q, kv, k_cache, v_cache, q_gain, k_gain, sinks, *, tq, mask_history):
    B, T, _ = q.shape
    return pl.pallas_call(
        functools.partial(_swa_kernel, tq=tq, mask_history=mask_history),
        out_shape=[jax.ShapeDtypeStruct((B, T, SWA_Q_DIM), F32),
                   jax.ShapeDtypeStruct((B, WINDOW, SWA_KV_DIM), F32),
                   jax.ShapeDtypeStruct((B, WINDOW, SWA_KV_DIM), F32)],
        grid=(B, T // tq),
        in_specs=[pl.BlockSpec((1, tq, SWA_Q_DIM), lambda b, c: (b, c, 0)),
                  pl.BlockSpec((1, tq, 2 * SWA_KV_DIM), lambda b, c: (b, c, 0)),
                  pl.BlockSpec((1, WINDOW, SWA_KV_DIM), lambda b, c: (b, 0, 0)),
                  pl.BlockSpec((1, WINDOW, SWA_KV_DIM), lambda b, c: (b, 0, 0)),
                  _const_spec((1, SWA_HD)), _const_spec((1, SWA_HD)), _const_spec((1, SWA_HQ))],
        out_specs=[pl.BlockSpec((1, tq, SWA_Q_DIM), lambda b, c: (b, c, 0)),
                   pl.BlockSpec((1, WINDOW, SWA_KV_DIM), lambda b, c: (b, 0, 0)),
                   pl.BlockSpec((1, WINDOW, SWA_KV_DIM), lambda b, c: (b, 0, 0))],
        scratch_shapes=[pltpu.VMEM((WINDOW + tq, SWA_KV_DIM), F32),
                        pltpu.VMEM((WINDOW + tq, SWA_KV_DIM), F32)],
        compiler_params=_params(("parallel", "arbitrary")),
        name="swa",
    )(q, kv, k_cache, v_cache, q_gain, k_gain, sinks)


def _layer(x, mod, conv_prefix, s0, k_cache, v_cache, wts, *, bb, tt, pbb, ptt, tq, mask_history):
    (norm_ffn1, ffn1_in, ffn1_out, norm_mix, w_main, w_ab, conv_w, alog_pad, dtb_pad, gnorm,
     q_gain, k_gain, sinks, b_merge, w_o, norm_ffn2, ffn2_in, ffn2_out, tf) = wts
    B, T, d = x.shape
    x = _ffn(x, mod[0:3], norm_ffn1, ffn1_in, ffn1_out, bb=bb, tt=tt, tf=tf)
    conv_in, z, q_s, kv_s, gates, ab = _inproj(x, mod[3:6], norm_mix, w_main, w_ab, bb=pbb, tt=ptt)
    o_g, new_s, new_conv = _gdn(conv_in, conv_prefix, ab, z, s0, conv_w, alog_pad, dtb_pad, gnorm)
    o_s, new_k, new_v = _swa(q_s, kv_s, k_cache, v_cache, q_gain, k_gain, sinks,
                             tq=tq, mask_history=mask_history)
    x = _mix_ffn(x, o_g, o_s, gates, b_merge, w_o, mod[3:6], mod[6:9], norm_ffn2, ffn2_in, ffn2_out,
                 bb=bb, tt=tt, tf=tf)
    new_k = new_k.reshape(B, WINDOW, SWA_HKV, SWA_HD)
    new_v = new_v.reshape(B, WINDOW, SWA_HKV, SWA_HD)
    return x, new_conv, new_s, new_k, new_v


def _pack_gate_up(w_in, tf):
    d, two_f = w_in.shape
    f = two_f // 2
    g = w_in[:, :f].reshape(d, f // tf, tf)
    u = w_in[:, f:].reshape(d, f // tf, tf)
    return jnp.concatenate([g, u], axis=-1).reshape(d, two_f).astype(BF16)


def _pad_lanes(v):
    return jnp.pad(v.astype(F32), (0, LANES - v.shape[0])).reshape(1, LANES)


def kernel(x_prompt, x_sample, state_gdn_conv, state_gdn, cache_swa_k, cache_swa_v, c_prompt, c_sample, w_ada, b_ada, norm_ffn1, ffn1_w_in, ffn1_w_out, norm_mix, w_in, gdn_conv_w, gdn_a_log, gdn_dt_bias, gdn_norm, swa_q_norm, swa_k_norm, swa_sinks, b_merge, w_out, norm_ffn2, ffn2_w_in, ffn2_w_out):
    depth = w_ada.shape[0]
    bp, tp, d = x_prompt.shape
    bs, ts, _ = x_sample.shape
    tf = 256
    yp, ys = x_prompt, x_sample
    outs_p, outs_s = [], []
    for l in range(depth):
        mod = _adaln(jnp.concatenate([c_prompt, c_sample], axis=0), w_ada[l], b_ada[l])
        mod = mod.reshape(N_MOD, bp + bs, 1, d)
        wl = w_in[l]
        o_conv, o_z = GDN_CONV_DIM, GDN_CONV_DIM + GDN_V_DIM
        o_a, o_b = o_z, o_z + GDN_HEADS
        o_q = o_b + GDN_HEADS
        o_k = o_q + SWA_Q_DIM
        o_g = o_k + 2 * SWA_KV_DIM
        w_main = jnp.concatenate([wl[:, :o_z], wl[:, o_q:]], axis=1).astype(BF16)
        pad = jnp.zeros((d, LANES - GDN_HEADS), wl.dtype)
        w_ab = jnp.concatenate([wl[:, o_a:o_b], pad, wl[:, o_b:o_q], pad], axis=1).astype(BF16)
        wts = (norm_ffn1[l].reshape(1, d), _pack_gate_up(ffn1_w_in[l], tf), ffn1_w_out[l].astype(BF16),
               norm_mix[l].reshape(1, d), w_main, w_ab, gdn_conv_w[l],
               _pad_lanes(gdn_a_log[l]), _pad_lanes(gdn_dt_bias[l]), gdn_norm[l].reshape(1, GDN_DV),
               swa_q_norm[l].reshape(1, SWA_HD), swa_k_norm[l].reshape(1, SWA_HD),
               swa_sinks[l].reshape(1, SWA_HQ).astype(F32), b_merge[l].reshape(1, 2 * d),
               w_out[l].astype(BF16), norm_ffn2[l].reshape(1, d),
               _pack_gate_up(ffn2_w_in[l], tf), ffn2_w_out[l].astype(BF16), tf)
        zero_conv = jnp.zeros((bp, CONV_W - 1, GDN_CONV_DIM), F32)
        zero_s = jnp.zeros((bp, GDN_HEADS, GDN_DK, GDN_DV), F32)
        zero_kv = jnp.zeros((bp, WINDOW, SWA_KV_DIM), F32)
        yp, *rest_p = _layer(yp, mod[:, :bp], zero_conv, zero_s, zero_kv, zero_kv, wts,
                             bb=1, tt=512, pbb=1, ptt=256, tq=256, mask_history=True)
        ys, *rest_s = _layer(ys, mod[:, bp:], state_gdn_conv[l], state_gdn[l],
                             cache_swa_k[l].reshape(bs, WINDOW, SWA_KV_DIM),
                             cache_swa_v[l].reshape(bs, WINDOW, SWA_KV_DIM), wts,
                             bb=8, tt=ts, pbb=4, ptt=ts, tq=ts, mask_history=False)
        outs_p.append(rest_p)
        outs_s.append(rest_s)
    stack = lambda outs, i: jnp.stack([o[i] for o in outs])
    return (yp, ys,
            stack(outs_p, 0), stack(outs_p, 1), stack(outs_p, 2), stack(outs_p, 3),
            stack(outs_s, 0), stack(outs_s, 1), stack(outs_s, 2), stack(outs_s, 3))
```

```python
import functools

import jax
import jax.numpy as jnp
from jax import lax
from jax.experimental import pallas as pl
from jax.experimental.pallas import tpu as pltpu

F32 = jnp.float32
BF16 = jnp.bfloat16

CHUNK = 64
GDN_HEADS = 8
GDN_DK = 128
GDN_DV = 128
GDN_QK_DIM = GDN_HEADS * GDN_DK
GDN_V_DIM = GDN_HEADS * GDN_DV
GDN_CONV_DIM = 2 * GDN_QK_DIM + GDN_V_DIM
CONV_W = 4
SWA_HQ = 16
SWA_HKV = 4
SWA_HD = 64
SWA_GROUP = SWA_HQ // SWA_HKV
SWA_Q_DIM = SWA_HQ * SWA_HD
SWA_KV_DIM = SWA_HKV * SWA_HD
WINDOW = 128
N_MOD = 9
EPS = 1e-6
LANES = 128
VMEM_LIMIT_BYTES = 60 * 1024 * 1024


def _dot(a, b):
    return jnp.dot(a, b, preferred_element_type=F32)


def _dot_nt(a, b):
    return lax.dot_general(a, b, (((1,), (1,)), ((), ())), preferred_element_type=F32)


def _dot_tn(a, b, precision=None):
    return lax.dot_general(a, b, (((0,), (0,)), ((), ())), preferred_element_type=F32,
                           precision=precision)


def _silu(x):
    return x * jax.nn.sigmoid(x)


def _softplus(x):
    return jnp.maximum(x, 0.0) + jnp.log1p(jnp.exp(-jnp.abs(x)))


def _modulate(x, gain, shift, scale):
    y = x * lax.rsqrt(jnp.mean(x * x, axis=-1, keepdims=True) + EPS)
    return (y * gain) * (1 + scale) + shift


def _const_spec(shape):
    nd = len(shape)
    return pl.BlockSpec(shape, lambda *_: (0,) * nd, pipeline_mode=pl.Buffered(1))


def _params(semantics):
    return pltpu.CompilerParams(dimension_semantics=semantics, vmem_limit_bytes=VMEM_LIMIT_BYTES)


def _adaln_kernel(c_ref, w_ref, b_ref, o_ref):
    s = _silu(c_ref[...]).astype(BF16)
    o_ref[0] = _dot(s, w_ref[...].astype(BF16)) + b_ref[0]


def _adaln(c, w_ada, b_ada):
    nb, d = c.shape
    return pl.pallas_call(
        _adaln_kernel,
        out_shape=jax.ShapeDtypeStruct((N_MOD, nb, d), F32),
        grid=(N_MOD,),
        in_specs=[pl.BlockSpec((nb, d), lambda j: (0, 0)),
                  pl.BlockSpec((d, d), lambda j: (0, j)),
                  pl.BlockSpec((1, 1, d), lambda j: (j, 0, 0))],
        out_specs=pl.BlockSpec((1, nb, d), lambda j: (j, 0, 0)),
        compiler_params=_params(("parallel",)),
        name="adaln",
    )(c, w_ada, b_ada.reshape(N_MOD, 1, d))


def _ffn_body(x, mod_ref, gain_ref, win_ref, wout_ref, act_ref, tf):
    bb, tt, d = x.shape
    d_ff = wout_ref.shape[0]
    sh, sc, gt = mod_ref[0], mod_ref[1], mod_ref[2]
    h = _modulate(x, gain_ref[...], sh, sc).reshape(bb * tt, d).astype(BF16)
    for j in range(d_ff // tf):
        gu = _dot(h, win_ref[:, 2 * j * tf:2 * (j + 1) * tf])
        act_ref[:, j * tf:(j + 1) * tf] = (_silu(gu[:, :tf]) * gu[:, tf:]).astype(BF16)
    y = _dot(act_ref[...], wout_ref[...])
    return x + (0.5 * gt) * y.reshape(bb, tt, d)


def _ffn_kernel(x_ref, mod_ref, gain_ref, win_ref, wout_ref, o_ref, act_ref, *, tf):
    o_ref[...] = _ffn_body(x_ref[...], mod_ref, gain_ref, win_ref, wout_ref, act_ref, tf)


def _mix_ffn_kernel(x_ref, og_ref, os_ref, gates_ref, bm_ref, wo_ref, gt2_ref,
                    mod_ref, gain_ref, win_ref, wout_ref, o_ref, act_ref, *, tf):
    x = x_ref[...]
    bb, tt, d = x.shape
    gl = jax.nn.sigmoid(gates_ref[...] + bm_ref[...])
    mix = gl[..., :d] * og_ref[...] + gl[..., d:] * os_ref[...]
    y = _dot(mix.reshape(bb * tt, d).astype(BF16), wo_ref[...]).reshape(bb, tt, d)
    x1 = x + gt2_ref[0] * y
    o_ref[...] = _ffn_body(x1, mod_ref, gain_ref, win_ref, wout_ref, act_ref, tf)


def _tok_spec(bb, tt, width):
    return pl.BlockSpec((bb, tt, width), lambda b, t: (b, t, 0))


def _mod_spec(n, bb, d):
    return pl.BlockSpec((n, bb, 1, d), lambda b, t: (0, b, 0, 0))


def _ffn(x, mod3, gain, w_in_packed, w_out, *, bb, tt, tf):
    B, T, d = x.shape
    d_ff = w_out.shape[0]
    return pl.pallas_call(
        functools.partial(_ffn_kernel, tf=tf),
        out_shape=jax.ShapeDtypeStruct(x.shape, F32),
        grid=(B // bb, T // tt),
        in_specs=[_tok_spec(bb, tt, d), _mod_spec(3, bb, d), _const_spec((1, d)),
                  _const_spec((d, 2 * d_ff)), _const_spec((d_ff, d))],
        out_specs=_tok_spec(bb, tt, d),
        scratch_shapes=[pltpu.VMEM((bb * tt, d_ff), BF16)],
        compiler_params=_params(("parallel", "parallel")),
        name="ffn",
    )(x, mod3, gain, w_in_packed, w_out)


def _mix_ffn(x, og, os_, gates, b_merge, w_o, mod_mix, mod3, gain, w_in_packed, w_out, *, bb, tt, tf):
    B, T, d = x.shape
    d_ff = w_out.shape[0]
    return pl.pallas_call(
        functools.partial(_mix_ffn_kernel, tf=tf),
        out_shape=jax.ShapeDtypeStruct(x.shape, F32),
        grid=(B // bb, T // tt),
        in_specs=[_tok_spec(bb, tt, d), _tok_spec(bb, tt, d), _tok_spec(bb, tt, d),
                  _tok_spec(bb, tt, 2 * d), _const_spec((1, 2 * d)), _const_spec((d, d)),
                  pl.BlockSpec((1, bb, 1, d), lambda b, t: (2, b, 0, 0)),
                  _mod_spec(3, bb, d), _const_spec((1, d)),
                  _const_spec((d, 2 * d_ff)), _const_spec((d_ff, d))],
        out_specs=_tok_spec(bb, tt, d),
        scratch_shapes=[pltpu.VMEM((bb * tt, d_ff), BF16)],
        compiler_params=_params(("parallel", "parallel")),
        name="mix_ffn",
    )(x, og, os_, gates, b_merge, w_o, mod_mix, mod3, gain, w_in_packed, w_out)


_PROJ_STEP = 512
_AB_WIDTH = 2 * LANES


def _inproj_kernel(x_ref, mod_ref, gain_ref, w_ref, wab_ref,
                   conv_ref, z_ref, q_ref, kv_ref, gates_ref, ab_ref):
    x = x_ref[...]
    bb, tt, d = x.shape
    h = _modulate(x, gain_ref[...], mod_ref[0], mod_ref[1]).reshape(bb * tt, d).astype(BF16)
    col = 0
    for ref in (conv_ref, z_ref, q_ref, kv_ref, gates_ref):
        width = ref.shape[-1]
        for lo in range(0, width, _PROJ_STEP):
            hi = min(lo + _PROJ_STEP, width)
            ref[:, :, lo:hi] = _dot(h, w_ref[:, col + lo:col + hi]).reshape(bb, tt, hi - lo)
        col += width
    ab_ref[...] = _dot(h, wab_ref[...]).reshape(bb, tt, _AB_WIDTH)


def _inproj(x, mod3, gain, w_main, w_ab, *, bb, tt):
    B, T, d = x.shape
    widths = (GDN_CONV_DIM, GDN_V_DIM, SWA_Q_DIM, 2 * SWA_KV_DIM, 2 * d, _AB_WIDTH)
    return pl.pallas_call(
        _inproj_kernel,
        out_shape=[jax.ShapeDtypeStruct((B, T, w), F32) for w in widths],
        grid=(B // bb, T // tt),
        in_specs=[_tok_spec(bb, tt, d), _mod_spec(3, bb, d), _const_spec((1, d)),
                  _const_spec(w_main.shape), _const_spec(w_ab.shape)],
        out_specs=[_tok_spec(bb, tt, w) for w in widths],
        compiler_params=_params(("parallel", "parallel")),
        name="inproj",
    )(x, mod3, gain, w_main, w_ab)


_XP_PAD = 8


def _pair_masks(size):
    row = lax.broadcasted_iota(jnp.int32, (size, 2 * size), 0)
    lane = lax.broadcasted_iota(jnp.int32, (size, 2 * size), 1)
    col = lane % size
    merges = []
    s = 1
    while s < size:
        rb, cb = row // s, col // s
        merges.append((rb == cb + 1) & (rb % 2 == 1))
        s *= 2
    return dict(left=lane < size, eye=row == col, incl=row >= col, strict=row > col, merges=merges)


def _block_diag_pair(m, left):
    zero = jnp.zeros_like(m)
    return jnp.concatenate([jnp.where(left, m, zero), jnp.where(left, zero, m)], axis=0)


def _block_diag(a, b):
    za = jnp.zeros_like(a)
    return jnp.concatenate([jnp.concatenate([a, za], axis=1), jnp.concatenate([za, b], axis=1)], axis=0)


def _gdn_kernel(conv_ref, prefix_ref, ab_ref, z_ref, s0_ref, convw_ref, alog_ref, dtb_ref, gnorm_ref,
                o_ref, sout_ref, convout_ref, s_scr, xp_scr, *, cps):
    c = pl.program_id(1)
    L = CHUNK
    tg = cps * L
    npair = GDN_HEADS // 2
    hp = lax.Precision.HIGHEST

    @pl.when(c == 0)
    def _():
        s_scr[...] = s0_ref[0]
        xp_scr[_XP_PAD - (CONV_W - 1):_XP_PAD, :] = prefix_ref[0]

    x = conv_ref[0]
    xp_scr[_XP_PAD:_XP_PAD + tg, :] = x
    w = convw_ref[...]
    y = x * w[CONV_W - 1:CONV_W]
    for i in range(CONV_W - 1):
        lo = _XP_PAD - (CONV_W - 1) + i
        y = y + xp_scr[lo:lo + tg, :] * w[i:i + 1]
    y = _silu(y)
    tail = x[tg - _XP_PAD:, :]
    xp_scr[0:_XP_PAD, :] = tail

    ab = ab_ref[0]
    g = -jnp.exp(alog_ref[...]) * _softplus(ab[:, :LANES] + dtb_ref[...])
    beta = jax.nn.sigmoid(ab[:, LANES:])
    mk = _pair_masks(L)
    left, incl, strict, merges = mk["left"], mk["incl"], mk["strict"], mk["merges"]
    row64 = lax.broadcasted_iota(jnp.int32, (L, L), 0)
    col64 = lax.broadcasted_iota(jnp.int32, (L, L), 1)
    incl64 = (row64 >= col64).astype(F32)
    upper2 = jnp.where(strict, 0.0, 1.0)
    gnorm = gnorm_ref[...]
    z = z_ref[0]

    def head_cols(off, i, h):
        return y[i * L:(i + 1) * L, off + h * GDN_DK:off + (h + 1) * GDN_DK]

    gc = [jnp.dot(incl64, g[i * L:(i + 1) * L], preferred_element_type=F32, precision=hp)
          for i in range(cps)]
    g2 = [_dot_tn(g[i * L:(i + 1) * L], upper2, precision=hp) for i in range(cps)]

    units = [(i, p) for i in range(cps) for p in range(npair)]
    heads = [(i, h) for i in range(cps) for h in range(GDN_HEADS)]

    qn, kn, vb, eg, kdec, dl = {}, {}, {}, {}, {}, {}
    for (i, h) in heads:
        qh, kh = head_cols(0, i, h), head_cols(GDN_QK_DIM, i, h)
        qn[i, h] = qh * lax.rsqrt(jnp.sum(qh * qh, axis=-1, keepdims=True) + EPS) * (GDN_DK ** -0.5)
        kn[i, h] = kh * lax.rsqrt(jnp.sum(kh * kh, axis=-1, keepdims=True) + EPS)
        gch = gc[i][:, h:h + 1]
        gl = gc[i][L - 1:L, h:h + 1]
        bt = beta[i * L:(i + 1) * L, h:h + 1]
        eg[i, h] = jnp.exp(gch)
        vb[i, h] = jnp.concatenate([head_cols(2 * GDN_QK_DIM, i, h) * bt,
                                    kn[i, h] * (bt * eg[i, h])], axis=-1).astype(BF16)
        kdec[i, h] = (kn[i, h] * jnp.exp(gl - gch)).astype(BF16)
        dl[i, h] = jnp.exp(gl)

    gamma, n_p, qkg = {}, {}, {}
    for (i, p) in units:
        a, b = 2 * p, 2 * p + 1
        gc_p = jnp.where(left, gc[i][:, a:a + 1], gc[i][:, b:b + 1])
        gr_p = jnp.where(left[0:1], g2[i][a:a + 1, :], g2[i][b:b + 1, :])
        gamma[i, p] = jnp.exp(jnp.where(incl, gc_p - gr_p, -jnp.inf))
    for (i, p) in units:
        a, b = 2 * p, 2 * p + 1
        ka, kb = kn[i, a].astype(BF16), kn[i, b].astype(BF16)
        lhs = jnp.concatenate([jnp.concatenate([ka, kb], axis=1),
                               jnp.concatenate([qn[i, a].astype(BF16), qn[i, b].astype(BF16)], axis=1)],
                              axis=0)
        res = _dot_nt(lhs, _block_diag(ka, kb))
        bt_p = jnp.where(left, beta[i * L:(i + 1) * L, a:a + 1], beta[i * L:(i + 1) * L, b:b + 1])
        n_p[i, p] = jnp.where(strict, bt_p * res[:L] * gamma[i, p], 0.0)
        qkg[i, p] = (res[L:] * gamma[i, p]).astype(BF16)

    t = {u: jnp.where(mk["eye"], 1.0, 0.0) - jnp.where(merges[0], n_p[u], 0.0) for u in units}
    for m in merges[1:]:
        ct = {u: _dot(jnp.where(m, n_p[u], 0.0).astype(BF16), _block_diag_pair(t[u].astype(BF16), left))
              for u in units}
        t = {u: t[u] - _dot(t[u].astype(BF16), _block_diag_pair(ct[u].astype(BF16), left)) for u in units}

    uw, wq = {}, {}
    for (i, p) in units:
        a, b = 2 * p, 2 * p + 1
        uw[i, p] = _dot(t[i, p].astype(BF16), _block_diag(vb[i, a], vb[i, b]))
    for (i, p) in units:
        a, b = 2 * p, 2 * p + 1
        wq[i, p] = jnp.concatenate(
            [jnp.concatenate([uw[i, p][:, GDN_DV:2 * GDN_DV], uw[i, p][:, 3 * GDN_DV:]], axis=1),
             jnp.concatenate([qn[i, a] * eg[i, a], qn[i, b] * eg[i, b]], axis=1)], axis=0).astype(BF16)

    s_cur = [s_scr[h] for h in range(GDN_HEADS)]
    o_out = {}
    for i in range(cps):
        r, vn = {}, {}
        for p in range(npair):
            a, b = 2 * p, 2 * p + 1
            r[p] = _dot(wq[i, p], _block_diag(s_cur[a].astype(BF16), s_cur[b].astype(BF16)))
        for p in range(npair):
            a, b = 2 * p, 2 * p + 1
            vn[a] = (uw[i, p][:, :GDN_DV] - r[p][:L, :GDN_DV]).astype(BF16)
            vn[b] = (uw[i, p][:, 2 * GDN_DV:3 * GDN_DV] - r[p][:L, GDN_DV:]).astype(BF16)
        for p in range(npair):
            a, b = 2 * p, 2 * p + 1
            o2 = _dot(qkg[i, p], _block_diag(vn[a], vn[b]))
            o_out[i, a] = r[p][L:, :GDN_DV] + o2[:, :GDN_DV]
            o_out[i, b] = r[p][L:, GDN_DV:] + o2[:, GDN_DV:]
        for h in range(GDN_HEADS):
            s_cur[h] = dl[i, h] * s_cur[h] + _dot_tn(kdec[i, h], vn[h])

    for (i, h) in heads:
        o = o_out[i, h]
        on = o * lax.rsqrt(jnp.mean(o * o, axis=-1, keepdims=True) + EPS) * gnorm
        o_ref[0, i * L:(i + 1) * L, h * GDN_DV:(h + 1) * GDN_DV] = on * _silu(
            z[i * L:(i + 1) * L, h * GDN_DV:(h + 1) * GDN_DV])
    for h in range(GDN_HEADS):
        s_scr[h] = s_cur[h]

    @pl.when(c == pl.num_programs(1) - 1)
    def _():
        sout_ref[0] = s_scr[...]
        convout_ref[0] = tail[_XP_PAD - (CONV_W - 1):, :]


def _gdn(conv_in, prefix, ab, z, s0, conv_w, alog_pad, dtb_pad, gnorm, *, cps):
    B, T, cd = conv_in.shape
    tg = cps * CHUNK
    return pl.pallas_call(
        functools.partial(_gdn_kernel, cps=cps),
        out_shape=[jax.ShapeDtypeStruct((B, T, GDN_V_DIM), F32),
                   jax.ShapeDtypeStruct(s0.shape, F32),
                   jax.ShapeDtypeStruct((B, CONV_W - 1, cd), F32)],
        grid=(B, T // tg),
        in_specs=[pl.BlockSpec((1, tg, cd), lambda b, c: (b, c, 0)),
                  pl.BlockSpec((1, CONV_W - 1, cd), lambda b, c: (b, 0, 0)),
                  pl.BlockSpec((1, tg, _AB_WIDTH), lambda b, c: (b, c, 0)),
                  pl.BlockSpec((1, tg, GDN_V_DIM), lambda b, c: (b, c, 0)),
                  pl.BlockSpec((1,) + s0.shape[1:], lambda b, c: (b, 0, 0, 0)),
                  _const_spec(conv_w.shape), _const_spec((1, LANES)), _const_spec((1, LANES)),
                  _const_spec((1, GDN_DV))],
        out_specs=[pl.BlockSpec((1, tg, GDN_V_DIM), lambda b, c: (b, c, 0)),
                   pl.BlockSpec((1,) + s0.shape[1:], lambda b, c: (b, 0, 0, 0)),
                   pl.BlockSpec((1, CONV_W - 1, cd), lambda b, c: (b, 0, 0))],
        scratch_shapes=[pltpu.VMEM(s0.shape[1:], F32),
                        pltpu.VMEM((_XP_PAD + tg, cd), F32)],
        compiler_params=_params(("parallel", "arbitrary")),
        name="gdn",
    )(conv_in, prefix, ab, z, s0, conv_w, alog_pad, dtb_pad, gnorm)


def _head_rms(x, gain):
    return x * lax.rsqrt(jnp.mean(x * x, axis=-1, keepdims=True) + EPS) * gain


def _swa_kernel(q_ref, kv_ref, kc_ref, vc_ref, qg_ref, kg_ref, sink_ref,
                o_ref, kout_ref, vout_ref, kwin, vwin, *, tq, mask_history):
    c = pl.program_id(1)
    L = CHUNK
    span = WINDOW + L

    @pl.when(c == 0)
    def _():
        kwin[0:WINDOW, :] = kc_ref[0]
        vwin[0:WINDOW, :] = vc_ref[0]

    kv = kv_ref[0]
    kg = kg_ref[...]
    for hk in range(SWA_HKV):
        kwin[WINDOW:WINDOW + tq, hk * SWA_HD:(hk + 1) * SWA_HD] = _head_rms(
            kv[:, hk * SWA_HD:(hk + 1) * SWA_HD], kg)
    vwin[WINDOW:WINDOW + tq, :] = kv[:, SWA_KV_DIM:]

    qg = qg_ref[...]
    scale = SWA_HD ** -0.5
    rows = SWA_GROUP * L
    sink_col = [jnp.concatenate([jnp.broadcast_to(sink_ref[:, h:h + 1], (L, 1))
                                 for h in range(hk * SWA_GROUP, (hk + 1) * SWA_GROUP)], axis=0)
                for hk in range(SWA_HKV)]
    nj = tq // L

    def scores(j):
        q_all = q_ref[0, j * L:(j + 1) * L, :]
        out = []
        for hk in range(SWA_HKV):
            q_g = jnp.concatenate(
                [_head_rms(q_all[:, h * SWA_HD:(h + 1) * SWA_HD], qg) * scale
                 for h in range(hk * SWA_GROUP, (hk + 1) * SWA_GROUP)], axis=0)
            k_h = kwin[j * L:j * L + span, hk * SWA_HD:(hk + 1) * SWA_HD].astype(BF16)
            out.append(_dot_nt(q_g.astype(BF16), k_h))
        return out

    s_next = scores(0)
    for j in range(nj):
        s = s_next
        if j + 1 < nj:
            s_next = scores(j + 1)
        if mask_history:
            kpos = c * tq + (j * L - WINDOW) + lax.broadcasted_iota(jnp.int32, (rows, span), 1)
            valid = kpos >= 0
        pv = []
        for hk in range(SWA_HKV):
            sc = jnp.where(valid, s[hk], -jnp.inf) if mask_history else s[hk]
            m = jnp.maximum(jnp.max(sc, axis=-1, keepdims=True), sink_col[hk])
            p = jnp.exp(sc - m)
            den = jnp.sum(p, axis=-1, keepdims=True) + jnp.exp(sink_col[hk] - m)
            v_h = vwin[j * L:j * L + span, hk * SWA_HD:(hk + 1) * SWA_HD].astype(BF16)
            pv.append(_dot(p.astype(BF16), v_h) / den)
        for hk in range(SWA_HKV):
            for pair in range(SWA_GROUP // 2):
                lo = (hk * SWA_GROUP + 2 * pair) * SWA_HD
                o_ref[0, j * L:(j + 1) * L, lo:lo + 2 * SWA_HD] = jnp.concatenate(
                    [pv[hk][(2 * pair) * L:(2 * pair + 1) * L], pv[hk][(2 * pair + 1) * L:(2 * pair + 2) * L]],
                    axis=-1)

    knew = kwin[tq:tq + WINDOW, :]
    vnew = vwin[tq:tq + WINDOW, :]
    kwin[0:WINDOW, :] = knew
    vwin[0:WINDOW, :] = vnew

    @pl.when(c == pl.num_programs(1) - 1)
    def _():
        kout_ref[0] = knew
        vout_ref[0] = vnew


def _swa(q, kv, k_cache, v_cache, q_gain, k_gain, sinks, *, tq, mask_history):
    B, T, _ = q.shape
    return pl.pallas_call(
        functools.partial(_swa_kernel, tq=tq, mask_history=mask_history),
        out_shape=[jax.ShapeDtypeStruct((B, T, SWA_Q_DIM), F32),
                   jax.ShapeDtypeStruct((B, WINDOW, SWA_KV_DIM), F32),
                   jax.ShapeDtypeStruct((B, WINDOW, SWA_KV_DIM), F32)],
        grid=(B, T // tq),
        in_specs=[pl.BlockSpec((1, tq, SWA_Q_DIM), lambda b, c: (b, c, 0)),
                  pl.BlockSpec((1, tq, 2 * SWA_KV_DIM), lambda b, c: (b, c, 0)),
                  pl.BlockSpec((1, WINDOW, SWA_KV_DIM), lambda b, c: (b, 0, 0)),
                  pl.BlockSpec((1, WINDOW, SWA_KV_DIM), lambda b, c: (b, 0, 0)),
                  _const_spec((1, SWA_HD)), _const_spec((1, SWA_HD)), _const_spec((1, SWA_HQ))],
        out_specs=[pl.BlockSpec((1, tq, SWA_Q_DIM), lambda b, c: (b, c, 0)),
                   pl.BlockSpec((1, WINDOW, SWA_KV_DIM), lambda b, c: (b, 0, 0)),
                   pl.BlockSpec((1, WINDOW, SWA_KV_DIM), lambda b, c: (b, 0, 0))],
        scratch_shapes=[pltpu.VMEM((WINDOW + tq, SWA_KV_DIM), F32),
                        pltpu.VMEM((WINDOW + tq, SWA_KV_DIM), F32)],
        compiler_params=_params(("parallel", "arbitrary")),
        name="swa",
    )(q, kv, k_cache, v_cache, q_gain, k_gain, sinks)


def _layer(x, mod, conv_prefix, s0, k_cache, v_cache, wts, *, bb, tt, pbb, ptt, cps, tq, mask_history):
    (norm_ffn1, ffn1_in, ffn1_out, norm_mix, w_main, w_ab, conv_w, alog_pad, dtb_pad, gnorm,
     q_gain, k_gain, sinks, b_merge, w_o, norm_ffn2, ffn2_in, ffn2_out, tf) = wts
    B, T, d = x.shape
    x = _ffn(x, mod[0:3], norm_ffn1, ffn1_in, ffn1_out, bb=bb, tt=tt, tf=tf)
    conv_in, z, q_s, kv_s, gates, ab = _inproj(x, mod[3:6], norm_mix, w_main, w_ab, bb=pbb, tt=ptt)
    o_g, new_s, new_conv = _gdn(conv_in, conv_prefix, ab, z, s0, conv_w, alog_pad, dtb_pad, gnorm, cps=cps)
    o_s, new_k, new_v = _swa(q_s, kv_s, k_cache, v_cache, q_gain, k_gain, sinks,
                             tq=tq, mask_history=mask_history)
    x = _mix_ffn(x, o_g, o_s, gates, b_merge, w_o, mod[3:6], mod[6:9], norm_ffn2, ffn2_in, ffn2_out,
                 bb=bb, tt=tt, tf=tf)
    new_k = new_k.reshape(B, WINDOW, SWA_HKV, SWA_HD)
    new_v = new_v.reshape(B, WINDOW, SWA_HKV, SWA_HD)
    return x, new_conv, new_s, new_k, new_v


def _pack_gate_up(w_in, tf):
    d, two_f = w_in.shape
    f = two_f // 2
    g = w_in[:, :f].reshape(d, f // tf, tf)
    u = w_in[:, f:].reshape(d, f // tf, tf)
    return jnp.concatenate([g, u], axis=-1).reshape(d, two_f).astype(BF16)


def _pad_lanes(v):
    return jnp.pad(v.astype(F32), (0, LANES - v.shape[0])).reshape(1, LANES)


def kernel(x_prompt, x_sample, state_gdn_conv, state_gdn, cache_swa_k, cache_swa_v, c_prompt, c_sample, w_ada, b_ada, norm_ffn1, ffn1_w_in, ffn1_w_out, norm_mix, w_in, gdn_conv_w, gdn_a_log, gdn_dt_bias, gdn_norm, swa_q_norm, swa_k_norm, swa_sinks, b_merge, w_out, norm_ffn2, ffn2_w_in, ffn2_w_out):
    depth = w_ada.shape[0]
    bp, tp, d = x_prompt.shape
    bs, ts, _ = x_sample.shape
    tf = 256
    yp, ys = x_prompt, x_sample
    outs_p, outs_s = [], []
    for l in range(depth):
        mod = _adaln(jnp.concatenate([c_prompt, c_sample], axis=0), w_ada[l], b_ada[l])
        mod = mod.reshape(N_MOD, bp + bs, 1, d)
        wl = w_in[l]
        o_conv, o_z = GDN_CONV_DIM, GDN_CONV_DIM + GDN_V_DIM
        o_a, o_b = o_z, o_z + GDN_HEADS
        o_q = o_b + GDN_HEADS
        o_k = o_q + SWA_Q_DIM
        o_g = o_k + 2 * SWA_KV_DIM
        w_main = jnp.concatenate([wl[:, :o_z], wl[:, o_q:]], axis=1).astype(BF16)
        pad = jnp.zeros((d, LANES - GDN_HEADS), wl.dtype)
        w_ab = jnp.concatenate([wl[:, o_a:o_b], pad, wl[:, o_b:o_q], pad], axis=1).astype(BF16)
        wts = (norm_ffn1[l].reshape(1, d), _pack_gate_up(ffn1_w_in[l], tf), ffn1_w_out[l].astype(BF16),
               norm_mix[l].reshape(1, d), w_main, w_ab, gdn_conv_w[l],
               _pad_lanes(gdn_a_log[l]), _pad_lanes(gdn_dt_bias[l]), gdn_norm[l].reshape(1, GDN_DV),
               swa_q_norm[l].reshape(1, SWA_HD), swa_k_norm[l].reshape(1, SWA_HD),
               swa_sinks[l].reshape(1, SWA_HQ).astype(F32), b_merge[l].reshape(1, 2 * d),
               w_out[l].astype(BF16), norm_ffn2[l].reshape(1, d),
               _pack_gate_up(ffn2_w_in[l], tf), ffn2_w_out[l].astype(BF16), tf)
        zero_conv = jnp.zeros((bp, CONV_W - 1, GDN_CONV_DIM), F32)
        zero_s = jnp.zeros((bp, GDN_HEADS, GDN_DK, GDN_DV), F32)
        zero_kv = jnp.zeros((bp, WINDOW, SWA_KV_DIM), F32)
        yp, *rest_p = _layer(yp, mod[:, :bp], zero_conv, zero_s, zero_kv, zero_kv, wts,
                             bb=1, tt=512, pbb=1, ptt=256, cps=2, tq=256, mask_history=True)
        ys, *rest_s = _layer(ys, mod[:, bp:], state_gdn_conv[l], state_gdn[l],
                             cache_swa_k[l].reshape(bs, WINDOW, SWA_KV_DIM),
                             cache_swa_v[l].reshape(bs, WINDOW, SWA_KV_DIM), wts,
                             bb=8, tt=ts, pbb=4, ptt=ts, cps=1, tq=ts, mask_history=False)
        outs_p.append(rest_p)
        outs_s.append(rest_s)
    stack = lambda outs, i: jnp.stack([o[i] for o in outs])
    return (yp, ys,
            stack(outs_p, 0), stack(outs_p, 1), stack(outs_p, 2), stack(outs_p, 3),
            stack(outs_s, 0), stack(outs_s, 1), stack(outs_s, 2), stack(outs_s, 3))
```

```python
import functools

import jax
import jax.numpy as jnp
from jax import lax
from jax.experimental import pallas as pl
from jax.experimental.pallas import tpu as pltpu

F32 = jnp.float32
BF16 = jnp.bfloat16

CHUNK = 64
GDN_HEADS = 8
GDN_DK = 128
GDN_DV = 128
GDN_QK_DIM = GDN_HEADS * GDN_DK
GDN_V_DIM = GDN_HEADS * GDN_DV
GDN_CONV_DIM = 2 * GDN_QK_DIM + GDN_V_DIM
CONV_W = 4
SWA_HQ = 16
SWA_HKV = 4
SWA_HD = 64
SWA_GROUP = SWA_HQ // SWA_HKV
SWA_Q_DIM = SWA_HQ * SWA_HD
SWA_KV_DIM = SWA_HKV * SWA_HD
WINDOW = 128
N_MOD = 9
EPS = 1e-6
LANES = 128
VMEM_LIMIT_BYTES = 60 * 1024 * 1024


def _dot(a, b):
    return jnp.dot(a, b, preferred_element_type=F32)


def _dot_nt(a, b):
    return lax.dot_general(a, b, (((1,), (1,)), ((), ())), preferred_element_type=F32)


def _dot_tn(a, b, precision=None):
    return lax.dot_general(a, b, (((0,), (0,)), ((), ())), preferred_element_type=F32,
                           precision=precision)


def _silu(x):
    return x * jax.nn.sigmoid(x)


def _softplus(x):
    return jnp.maximum(x, 0.0) + jnp.log1p(jnp.exp(-jnp.abs(x)))


def _modulate(x, gain, shift, scale):
    y = x * lax.rsqrt(jnp.mean(x * x, axis=-1, keepdims=True) + EPS)
    return (y * gain) * (1 + scale) + shift


def _const_spec(shape):
    nd = len(shape)
    return pl.BlockSpec(shape, lambda *_: (0,) * nd, pipeline_mode=pl.Buffered(1))


def _params(semantics):
    return pltpu.CompilerParams(dimension_semantics=semantics, vmem_limit_bytes=VMEM_LIMIT_BYTES)


def _adaln_kernel(c_ref, w_ref, b_ref, o_ref):
    s = _silu(c_ref[...]).astype(BF16)
    o_ref[0] = _dot(s, w_ref[...].astype(BF16)) + b_ref[0]


def _adaln(c, w_ada, b_ada):
    nb, d = c.shape
    return pl.pallas_call(
        _adaln_kernel,
        out_shape=jax.ShapeDtypeStruct((N_MOD, nb, d), F32),
        grid=(N_MOD,),
        in_specs=[pl.BlockSpec((nb, d), lambda j: (0, 0)),
                  pl.BlockSpec((d, d), lambda j: (0, j)),
                  pl.BlockSpec((1, 1, d), lambda j: (j, 0, 0))],
        out_specs=pl.BlockSpec((1, nb, d), lambda j: (j, 0, 0)),
        compiler_params=_params(("parallel",)),
        name="adaln",
    )(c, w_ada, b_ada.reshape(N_MOD, 1, d))


def _ffn_body(x, mod_ref, gain_ref, win_ref, wout_ref, act_ref, tf):
    bb, tt, d = x.shape
    d_ff = wout_ref.shape[0]
    sh, sc, gt = mod_ref[0], mod_ref[1], mod_ref[2]
    h = _modulate(x, gain_ref[...], sh, sc).reshape(bb * tt, d).astype(BF16)
    for j in range(d_ff // tf):
        gu = _dot(h, win_ref[:, 2 * j * tf:2 * (j + 1) * tf])
        act_ref[:, j * tf:(j + 1) * tf] = (_silu(gu[:, :tf]) * gu[:, tf:]).astype(BF16)
    y = _dot(act_ref[...], wout_ref[...])
    return x + (0.5 * gt) * y.reshape(bb, tt, d)


def _ffn_kernel(x_ref, mod_ref, gain_ref, win_ref, wout_ref, o_ref, act_ref, *, tf):
    o_ref[...] = _ffn_body(x_ref[...], mod_ref, gain_ref, win_ref, wout_ref, act_ref, tf)


def _mix_ffn_kernel(x_ref, og_ref, os_ref, gates_ref, bm_ref, wo_ref, gt2_ref,
                    mod_ref, gain_ref, win_ref, wout_ref, o_ref, act_ref, *, tf):
    x = x_ref[...]
    bb, tt, d = x.shape
    gl = jax.nn.sigmoid(gates_ref[...] + bm_ref[...])
    mix = gl * jnp.concatenate([og_ref[...], os_ref[...]], axis=-1)
    y = _dot(mix.reshape(bb * tt, 2 * d).astype(BF16), wo_ref[...]).reshape(bb, tt, d)
    x1 = x + gt2_ref[0] * y
    o_ref[...] = _ffn_body(x1, mod_ref, gain_ref, win_ref, wout_ref, act_ref, tf)


def _tok_spec(bb, tt, width):
    return pl.BlockSpec((bb, tt, width), lambda b, t: (b, t, 0))


def _mod_spec(n, bb, d):
    return pl.BlockSpec((n, bb, 1, d), lambda b, t: (0, b, 0, 0))


def _ffn(x, mod3, gain, w_in_packed, w_out, *, bb, tt, tf):
    B, T, d = x.shape
    d_ff = w_out.shape[0]
    return pl.pallas_call(
        functools.partial(_ffn_kernel, tf=tf),
        out_shape=jax.ShapeDtypeStruct(x.shape, F32),
        grid=(B // bb, T // tt),
        in_specs=[_tok_spec(bb, tt, d), _mod_spec(3, bb, d), _const_spec((1, d)),
                  _const_spec((d, 2 * d_ff)), _const_spec((d_ff, d))],
        out_specs=_tok_spec(bb, tt, d),
        scratch_shapes=[pltpu.VMEM((bb * tt, d_ff), BF16)],
        compiler_params=_params(("parallel", "parallel")),
        name="ffn",
    )(x, mod3, gain, w_in_packed, w_out)


def _mix_ffn(x, og, os_, gates, b_merge, w_o, mod_mix, mod3, gain, w_in_packed, w_out, *, bb, tt, tf):
    B, T, d = x.shape
    d_ff = w_out.shape[0]
    return pl.pallas_call(
        functools.partial(_mix_ffn_kernel, tf=tf),
        out_shape=jax.ShapeDtypeStruct(x.shape, F32),
        grid=(B // bb, T // tt),
        in_specs=[_tok_spec(bb, tt, d), _tok_spec(bb, tt, d), _tok_spec(bb, tt, d),
                  _tok_spec(bb, tt, 2 * d), _const_spec((1, 2 * d)), _const_spec((2 * d, d)),
                  pl.BlockSpec((1, bb, 1, d), lambda b, t: (2, b, 0, 0)),
                  _mod_spec(3, bb, d), _const_spec((1, d)),
                  _const_spec((d, 2 * d_ff)), _const_spec((d_ff, d))],
        out_specs=_tok_spec(bb, tt, d),
        scratch_shapes=[pltpu.VMEM((bb * tt, d_ff), BF16)],
        compiler_params=_params(("parallel", "parallel")),
        name="mix_ffn",
    )(x, og, os_, gates, b_merge, w_o, mod_mix, mod3, gain, w_in_packed, w_out)


_PROJ_STEP = 512
_AB_WIDTH = 2 * LANES


def _inproj_kernel(x_ref, mod_ref, gain_ref, w_ref, wab_ref,
                   conv_ref, z_ref, q_ref, kv_ref, gates_ref, ab_ref):
    x = x_ref[...]
    bb, tt, d = x.shape
    h = _modulate(x, gain_ref[...], mod_ref[0], mod_ref[1]).reshape(bb * tt, d).astype(BF16)
    col = 0
    for ref in (conv_ref, z_ref, q_ref, kv_ref, gates_ref):
        width = ref.shape[-1]
        for lo in range(0, width, _PROJ_STEP):
            hi = min(lo + _PROJ_STEP, width)
            ref[:, :, lo:hi] = _dot(h, w_ref[:, col + lo:col + hi]).reshape(bb, tt, hi - lo)
        col += width
    ab_ref[...] = _dot(h, wab_ref[...]).reshape(bb, tt, _AB_WIDTH)


def _inproj(x, mod3, gain, w_main, w_ab, *, bb, tt):
    B, T, d = x.shape
    widths = (GDN_CONV_DIM, GDN_V_DIM, SWA_Q_DIM, 2 * SWA_KV_DIM, 2 * d, _AB_WIDTH)
    return pl.pallas_call(
        _inproj_kernel,
        out_shape=[jax.ShapeDtypeStruct((B, T, w), F32) for w in widths],
        grid=(B // bb, T // tt),
        in_specs=[_tok_spec(bb, tt, d), _mod_spec(3, bb, d), _const_spec((1, d)),
                  _const_spec(w_main.shape), _const_spec(w_ab.shape)],
        out_specs=[_tok_spec(bb, tt, w) for w in widths],
        compiler_params=_params(("parallel", "parallel")),
        name="inproj",
    )(x, mod3, gain, w_main, w_ab)


_XP_PAD = 8


def _pair_masks(size):
    row = lax.broadcasted_iota(jnp.int32, (size, 2 * size), 0)
    lane = lax.broadcasted_iota(jnp.int32, (size, 2 * size), 1)
    col = lane % size
    merges = []
    s = 1
    while s < size:
        rb, cb = row // s, col // s
        merges.append((rb == cb + 1) & (rb % 2 == 1))
        s *= 2
    return dict(left=lane < size, eye=row == col, incl=row >= col, strict=row > col, merges=merges)


def _block_diag_pair(m, left):
    zero = jnp.zeros_like(m)
    return jnp.concatenate([jnp.where(left, m, zero), jnp.where(left, zero, m)], axis=0)


def _block_diag(a, b):
    za = jnp.zeros_like(a)
    return jnp.concatenate([jnp.concatenate([a, za], axis=1), jnp.concatenate([za, b], axis=1)], axis=0)


def _gdn_kernel(conv_ref, prefix_ref, ab_ref, z_ref, s0_ref, convw_ref, alog_ref, dtb_ref, gnorm_ref,
                o_ref, sout_ref, convout_ref, s_scr, xp_scr, *, cps):
    c = pl.program_id(1)
    L = CHUNK
    tg = cps * L
    npair = GDN_HEADS // 2
    hp = lax.Precision.HIGHEST

    @pl.when(c == 0)
    def _():
        s_scr[...] = s0_ref[0]
        xp_scr[_XP_PAD - (CONV_W - 1):_XP_PAD, :] = prefix_ref[0]

    x = conv_ref[0]
    xp_scr[_XP_PAD:_XP_PAD + tg, :] = x
    w = convw_ref[...]
    y = x * w[CONV_W - 1:CONV_W]
    for i in range(CONV_W - 1):
        lo = _XP_PAD - (CONV_W - 1) + i
        y = y + xp_scr[lo:lo + tg, :] * w[i:i + 1]
    y = _silu(y)
    tail = x[tg - _XP_PAD:, :]
    xp_scr[0:_XP_PAD, :] = tail

    ab = ab_ref[0]
    g = -jnp.exp(alog_ref[...]) * _softplus(ab[:, :LANES] + dtb_ref[...])
    beta = jax.nn.sigmoid(ab[:, LANES:])
    mk = _pair_masks(L)
    left, incl, strict, merges = mk["left"], mk["incl"], mk["strict"], mk["merges"]
    row64 = lax.broadcasted_iota(jnp.int32, (L, L), 0)
    col64 = lax.broadcasted_iota(jnp.int32, (L, L), 1)
    incl64 = (row64 >= col64).astype(F32)
    upper2 = jnp.where(strict, 0.0, 1.0)
    gnorm = gnorm_ref[...]
    z = z_ref[0]

    def head_cols(off, i, h):
        return y[i * L:(i + 1) * L, off + h * GDN_DK:off + (h + 1) * GDN_DK]

    gc = [jnp.dot(incl64, g[i * L:(i + 1) * L], preferred_element_type=F32, precision=hp)
          for i in range(cps)]
    g2 = [_dot_tn(g[i * L:(i + 1) * L], upper2, precision=hp) for i in range(cps)]

    units = [(i, p) for i in range(cps) for p in range(npair)]
    heads = [(i, h) for i in range(cps) for h in range(GDN_HEADS)]

    qn, kn, vb, eg, kdec, dl = {}, {}, {}, {}, {}, {}
    for (i, h) in heads:
        qh, kh = head_cols(0, i, h), head_cols(GDN_QK_DIM, i, h)
        qn[i, h] = qh * lax.rsqrt(jnp.sum(qh * qh, axis=-1, keepdims=True) + EPS) * (GDN_DK ** -0.5)
        kn[i, h] = kh * lax.rsqrt(jnp.sum(kh * kh, axis=-1, keepdims=True) + EPS)
        gch = gc[i][:, h:h + 1]
        gl = gc[i][L - 1:L, h:h + 1]
        bt = beta[i * L:(i + 1) * L, h:h + 1]
        eg[i, h] = jnp.exp(gch)
        vb[i, h] = jnp.concatenate([head_cols(2 * GDN_QK_DIM, i, h) * bt,
                                    kn[i, h] * (bt * eg[i, h])], axis=-1).astype(BF16)
        kdec[i, h] = (kn[i, h] * jnp.exp(gl - gch)).astype(BF16)
        dl[i, h] = jnp.exp(gl)

    gamma, n_p, qkg = {}, {}, {}
    for (i, p) in units:
        a, b = 2 * p, 2 * p + 1
        gc_p = jnp.where(left, gc[i][:, a:a + 1], gc[i][:, b:b + 1])
        gr_p = jnp.where(left[0:1], g2[i][a:a + 1, :], g2[i][b:b + 1, :])
        gamma[i, p] = jnp.exp(jnp.where(incl, gc_p - gr_p, -jnp.inf))
    for (i, p) in units:
        a, b = 2 * p, 2 * p + 1
        ka, kb = kn[i, a].astype(BF16), kn[i, b].astype(BF16)
        lhs = jnp.concatenate([jnp.concatenate([ka, kb], axis=1),
                               jnp.concatenate([qn[i, a].astype(BF16), qn[i, b].astype(BF16)], axis=1)],
                              axis=0)
        res = _dot_nt(lhs, _block_diag(ka, kb))
        bt_p = jnp.where(left, beta[i * L:(i + 1) * L, a:a + 1], beta[i * L:(i + 1) * L, b:b + 1])
        n_p[i, p] = jnp.where(strict, bt_p * res[:L] * gamma[i, p], 0.0)
        qkg[i, p] = (res[L:] * gamma[i, p]).astype(BF16)

    t = {u: jnp.where(mk["eye"], 1.0, 0.0) - jnp.where(merges[0], n_p[u], 0.0) for u in units}
    for m in merges[1:]:
        ct = {u: _dot(jnp.where(m, n_p[u], 0.0).astype(BF16), _block_diag_pair(t[u].astype(BF16), left))
              for u in units}
        t = {u: t[u] - _dot(t[u].astype(BF16), _block_diag_pair(ct[u].astype(BF16), left)) for u in units}

    uw, wq = {}, {}
    for (i, p) in units:
        a, b = 2 * p, 2 * p + 1
        uw[i, p] = _dot(t[i, p].astype(BF16), _block_diag(vb[i, a], vb[i, b]))
    for (i, p) in units:
        a, b = 2 * p, 2 * p + 1
        wq[i, p] = jnp.concatenate(
            [jnp.concatenate([uw[i, p][:, GDN_DV:2 * GDN_DV], uw[i, p][:, 3 * GDN_DV:]], axis=1),
             jnp.concatenate([qn[i, a] * eg[i, a], qn[i, b] * eg[i, b]], axis=1)], axis=0).astype(BF16)

    s_cur = [s_scr[h] for h in range(GDN_HEADS)]
    o_out = {}
    for i in range(cps):
        r, vn = {}, {}
        for p in range(npair):
            a, b = 2 * p, 2 * p + 1
            r[p] = _dot(wq[i, p], _block_diag(s_cur[a].astype(BF16), s_cur[b].astype(BF16)))
        for p in range(npair):
            a, b = 2 * p, 2 * p + 1
            vn[a] = (uw[i, p][:, :GDN_DV] - r[p][:L, :GDN_DV]).astype(BF16)
            vn[b] = (uw[i, p][:, 2 * GDN_DV:3 * GDN_DV] - r[p][:L, GDN_DV:]).astype(BF16)
        for p in range(npair):
            a, b = 2 * p, 2 * p + 1
            o2 = _dot(qkg[i, p], _block_diag(vn[a], vn[b]))
            o_out[i, a] = r[p][L:, :GDN_DV] + o2[:, :GDN_DV]
            o_out[i, b] = r[p][L:, GDN_DV:] + o2[:, GDN_DV:]
        for h in range(GDN_HEADS):
            s_cur[h] = dl[i, h] * s_cur[h] + _dot_tn(kdec[i, h], vn[h])

    for (i, h) in heads:
        o = o_out[i, h]
        on = o * lax.rsqrt(jnp.mean(o * o, axis=-1, keepdims=True) + EPS) * gnorm
        o_ref[0, i * L:(i + 1) * L, h * GDN_DV:(h + 1) * GDN_DV] = on * _silu(
            z[i * L:(i + 1) * L, h * GDN_DV:(h + 1) * GDN_DV])
    for h in range(GDN_HEADS):
        s_scr[h] = s_cur[h]

    @pl.when(c == pl.num_programs(1) - 1)
    def _():
        sout_ref[0] = s_scr[...]
        convout_ref[0] = tail[_XP_PAD - (CONV_W - 1):, :]


def _gdn(conv_in, prefix, ab, z, s0, conv_w, alog_pad, dtb_pad, gnorm, *, cps):
    B, T, cd = conv_in.shape
    tg = cps * CHUNK
    return pl.pallas_call(
        functools.partial(_gdn_kernel, cps=cps),
        out_shape=[jax.ShapeDtypeStruct((B, T, GDN_V_DIM), F32),
                   jax.ShapeDtypeStruct(s0.shape, F32),
                   jax.ShapeDtypeStruct((B, CONV_W - 1, cd), F32)],
        grid=(B, T // tg),
        in_specs=[pl.BlockSpec((1, tg, cd), lambda b, c: (b, c, 0)),
                  pl.BlockSpec((1, CONV_W - 1, cd), lambda b, c: (b, 0, 0)),
                  pl.BlockSpec((1, tg, _AB_WIDTH), lambda b, c: (b, c, 0)),
                  pl.BlockSpec((1, tg, GDN_V_DIM), lambda b, c: (b, c, 0)),
                  pl.BlockSpec((1,) + s0.shape[1:], lambda b, c: (b, 0, 0, 0)),
                  _const_spec(conv_w.shape), _const_spec((1, LANES)), _const_spec((1, LANES)),
                  _const_spec((1, GDN_DV))],
        out_specs=[pl.BlockSpec((1, tg, GDN_V_DIM), lambda b, c: (b, c, 0)),
                   pl.BlockSpec((1,) + s0.shape[1:], lambda b, c: (b, 0, 0, 0)),
                   pl.BlockSpec((1, CONV_W - 1, cd), lambda b, c: (b, 0, 0))],
        scratch_shapes=[pltpu.VMEM(s0.shape[1:], F32),
                        pltpu.VMEM((_XP_PAD + tg, cd), F32)],
        compiler_params=_params(("parallel", "arbitrary")),
        name="gdn",
    )(conv_in, prefix, ab, z, s0, conv_w, alog_pad, dtb_pad, gnorm)


def _split_bf16(x):
    hi = x.astype(BF16)
    return hi, (x - hi.astype(F32)).astype(BF16)


def _segment_rms(x, gain, ones2):
    hi, lo = _split_bf16(x * x)
    ss = _dot(jnp.concatenate([hi, lo], axis=1), ones2)
    return x * lax.rsqrt(ss * (1.0 / SWA_HD) + EPS) * gain


def _swa_kernel(q_ref, kv_ref, kc_ref, vc_ref, qg_ref, kg_ref, sink_ref,
                o_ref, kout_ref, vout_ref, kwin, vwin, *, tq, mask_history):
    c = pl.program_id(1)
    L = CHUNK
    span = WINDOW + L
    kvd = SWA_KV_DIM

    @pl.when(c == 0)
    def _():
        kwin[0:WINDOW, :] = kc_ref[0]
        vwin[0:WINDOW, :] = vc_ref[0]

    seg_r = lax.broadcasted_iota(jnp.int32, (kvd, kvd), 0) // SWA_HD
    seg_c = lax.broadcasted_iota(jnp.int32, (kvd, kvd), 1) // SWA_HD
    ones_bd = jnp.where(seg_r == seg_c, 1.0, 0.0).astype(BF16)
    ones2 = jnp.concatenate([ones_bd, ones_bd], axis=0)

    kv = kv_ref[0]
    kwin[WINDOW:WINDOW + tq, :] = _segment_rms(kv[:, :kvd], kg_ref[...], ones2)
    vwin[WINDOW:WINDOW + tq, :] = kv[:, kvd:]

    kw, vw = kwin[...], vwin[...]
    seg_w = lax.broadcasted_iota(jnp.int32, kw.shape, 1) // SWA_HD
    k_only = [jnp.where(seg_w == hk, kw, 0.0).astype(BF16) for hk in range(SWA_HKV)]
    v_only = [jnp.where(seg_w == hk, vw, 0.0).astype(BF16) for hk in range(SWA_HKV)]

    rows = SWA_GROUP * L
    scale = SWA_HD ** -0.5
    qgain = qg_ref[...] * scale
    sink_col = [jnp.concatenate([jnp.broadcast_to(sink_ref[:, h:h + 1], (L, 1))
                                 for h in range(hk * SWA_GROUP, (hk + 1) * SWA_GROUP)], axis=0)
                for hk in range(SWA_HKV)]
    seg_o = lax.broadcasted_iota(jnp.int32, (rows, kvd), 1) // SWA_HD

    for j in range(tq // L):
        q_stack = jnp.concatenate([q_ref[0, j * L:(j + 1) * L, g * kvd:(g + 1) * kvd]
                                   for g in range(SWA_GROUP)], axis=0)
        qn = _segment_rms(q_stack, qgain, ones2).astype(BF16)
        if mask_history:
            kpos = c * tq + (j * L - WINDOW) + lax.broadcasted_iota(jnp.int32, (rows, span), 1)
            valid = kpos >= 0
        acc = None
        den_full = None
        for hk in range(SWA_HKV):
            s = _dot_nt(qn, k_only[hk][j * L:j * L + span])
            if mask_history:
                s = jnp.where(valid, s, -jnp.inf)
            m = jnp.maximum(jnp.max(s, axis=-1, keepdims=True), sink_col[hk])
            p = jnp.exp(s - m)
            den = jnp.sum(p, axis=-1, keepdims=True) + jnp.exp(sink_col[hk] - m)
            pv = _dot(p.astype(BF16), v_only[hk][j * L:j * L + span])
            acc = pv if acc is None else acc + pv
            den_full = den if den_full is None else jnp.where(seg_o == hk, den, den_full)
        o = acc / den_full
        for g in range(SWA_GROUP):
            o_ref[0, j * L:(j + 1) * L, g * kvd:(g + 1) * kvd] = o[g * L:(g + 1) * L]

    knew = kwin[tq:tq + WINDOW, :]
    vnew = vwin[tq:tq + WINDOW, :]
    kwin[0:WINDOW, :] = knew
    vwin[0:WINDOW, :] = vnew

    @pl.when(c == pl.num_programs(1) - 1)
    def _():
        kout_ref[0] = knew
        vout_ref[0] = vnew


def _swa(q, kv, k_cache, v_cache, q_gain, k_gain, sinks, *, tq, mask_history):
    B, T, _ = q.shape
    return pl.pallas_call(
        functools.partial(_swa_kernel, tq=tq, mask_history=mask_history),
        out_shape=[jax.ShapeDtypeStruct((B, T, SWA_Q_DIM), F32),
                   jax.ShapeDtypeStruct((B, WINDOW, SWA_KV_DIM), F32),
                   jax.ShapeDtypeStruct((B, WINDOW, SWA_KV_DIM), F32)],
        grid=(B, T // tq),
        in_specs=[pl.BlockSpec((1, tq, SWA_Q_DIM), lambda b, c: (b, c, 0)),
                  pl.BlockSpec((1, tq, 2 * SWA_KV_DIM), lambda b, c: (b, c, 0)),
                  pl.BlockSpec((1, WINDOW, SWA_KV_DIM), lambda b, c: (b, 0, 0)),
                  pl.BlockSpec((1, WINDOW, SWA_KV_DIM), lambda b, c: (b, 0, 0)),
                  _const_spec((1, SWA_KV_DIM)), _const_spec((1, SWA_KV_DIM)), _const_spec((1, SWA_HQ))],
        out_specs=[pl.BlockSpec((1, tq, SWA_Q_DIM), lambda b, c: (b, c, 0)),
                   pl.BlockSpec((1, WINDOW, SWA_KV_DIM), lambda b, c: (b, 0, 0)),
                   pl.BlockSpec((1, WINDOW, SWA_KV_DIM), lambda b, c: (b, 0, 0))],
        scratch_shapes=[pltpu.VMEM((WINDOW + tq, SWA_KV_DIM), F32),
                        pltpu.VMEM((WINDOW + tq, SWA_KV_DIM), F32)],
        compiler_params=_params(("parallel", "arbitrary")),
        name="swa",
    )(q, kv, k_cache, v_cache, q_gain, k_gain, sinks)


def _layer(x, mod, conv_prefix, s0, k_cache, v_cache, wts, *, bb, tt, pbb, ptt, cps, tq, mask_history):
    (norm_ffn1, ffn1_in, ffn1_out, norm_mix, w_main, w_ab, conv_w, alog_pad, dtb_pad, gnorm,
     q_gain, k_gain, sinks, b_merge, w_o, norm_ffn2, ffn2_in, ffn2_out, tf) = wts
    B, T, d = x.shape
    x = _ffn(x, mod[0:3], norm_ffn1, ffn1_in, ffn1_out, bb=bb, tt=tt, tf=tf)
    conv_in, z, q_s, kv_s, gates, ab = _inproj(x, mod[3:6], norm_mix, w_main, w_ab, bb=pbb, tt=ptt)
    o_g, new_s, new_conv = _gdn(conv_in, conv_prefix, ab, z, s0, conv_w, alog_pad, dtb_pad, gnorm, cps=cps)
    o_s, new_k, new_v = _swa(q_s, kv_s, k_cache, v_cache, q_gain, k_gain, sinks,
                             tq=tq, mask_history=mask_history)
    x = _mix_ffn(x, o_g, o_s, gates, b_merge, w_o, mod[3:6], mod[6:9], norm_ffn2, ffn2_in, ffn2_out,
                 bb=bb, tt=tt, tf=tf)
    new_k = new_k.reshape(B, WINDOW, SWA_HKV, SWA_HD)
    new_v = new_v.reshape(B, WINDOW, SWA_HKV, SWA_HD)
    return x, new_conv, new_s, new_k, new_v


def _pack_gate_up(w_in, tf):
    d, two_f = w_in.shape
    f = two_f // 2
    g = w_in[:, :f].reshape(d, f // tf, tf)
    u = w_in[:, f:].reshape(d, f // tf, tf)
    return jnp.concatenate([g, u], axis=-1).reshape(d, two_f).astype(BF16)


def _group_major(w, axis):
    shape = w.shape
    split = shape[:axis] + (SWA_HKV, SWA_GROUP, SWA_HD) + shape[axis + 1:]
    return jnp.swapaxes(w.reshape(split), axis, axis + 1).reshape(shape)


def _pad_lanes(v):
    return jnp.pad(v.astype(F32), (0, LANES - v.shape[0])).reshape(1, LANES)


def kernel(x_prompt, x_sample, state_gdn_conv, state_gdn, cache_swa_k, cache_swa_v, c_prompt, c_sample, w_ada, b_ada, norm_ffn1, ffn1_w_in, ffn1_w_out, norm_mix, w_in, gdn_conv_w, gdn_a_log, gdn_dt_bias, gdn_norm, swa_q_norm, swa_k_norm, swa_sinks, b_merge, w_out, norm_ffn2, ffn2_w_in, ffn2_w_out):
    depth = w_ada.shape[0]
    bp, tp, d = x_prompt.shape
    bs, ts, _ = x_sample.shape
    tf = 256
    yp, ys = x_prompt, x_sample
    outs_p, outs_s = [], []
    for l in range(depth):
        mod = _adaln(jnp.concatenate([c_prompt, c_sample], axis=0), w_ada[l], b_ada[l])
        mod = mod.reshape(N_MOD, bp + bs, 1, d)
        wl = w_in[l]
        o_conv, o_z = GDN_CONV_DIM, GDN_CONV_DIM + GDN_V_DIM
        o_a, o_b = o_z, o_z + GDN_HEADS
        o_q = o_b + GDN_HEADS
        o_k = o_q + SWA_Q_DIM
        o_g = o_k + 2 * SWA_KV_DIM
        o_gb = o_g + d
        w_main = jnp.concatenate(
            [wl[:, :o_z], _group_major(wl[:, o_q:o_k], 1), wl[:, o_k:o_gb], _group_major(wl[:, o_gb:], 1)],
            axis=1).astype(BF16)
        pad = jnp.zeros((d, LANES - GDN_HEADS), wl.dtype)
        w_ab = jnp.concatenate([wl[:, o_a:o_b], pad, wl[:, o_b:o_q], pad], axis=1).astype(BF16)
        bm = jnp.concatenate([b_merge[l][:d], _group_major(b_merge[l][d:], 0)]).reshape(1, 2 * d)
        w_o2 = jnp.concatenate([w_out[l], _group_major(w_out[l], 0)], axis=0).astype(BF16)
        wts = (norm_ffn1[l].reshape(1, d), _pack_gate_up(ffn1_w_in[l], tf), ffn1_w_out[l].astype(BF16),
               norm_mix[l].reshape(1, d), w_main, w_ab, gdn_conv_w[l],
               _pad_lanes(gdn_a_log[l]), _pad_lanes(gdn_dt_bias[l]), gdn_norm[l].reshape(1, GDN_DV),
               jnp.tile(swa_q_norm[l], SWA_HKV).reshape(1, SWA_KV_DIM),
               jnp.tile(swa_k_norm[l], SWA_HKV).reshape(1, SWA_KV_DIM),
               swa_sinks[l].reshape(1, SWA_HQ).astype(F32), bm,
               w_o2, norm_ffn2[l].reshape(1, d),
               _pack_gate_up(ffn2_w_in[l], tf), ffn2_w_out[l].astype(BF16), tf)
        zero_conv = jnp.zeros((bp, CONV_W - 1, GDN_CONV_DIM), F32)
        zero_s = jnp.zeros((bp, GDN_HEADS, GDN_DK, GDN_DV), F32)
        zero_kv = jnp.zeros((bp, WINDOW, SWA_KV_DIM), F32)
        yp, *rest_p = _layer(yp, mod[:, :bp], zero_conv, zero_s, zero_kv, zero_kv, wts,
                             bb=1, tt=512, pbb=1, ptt=256, cps=2, tq=256, mask_history=True)
        ys, *rest_s = _layer(ys, mod[:, bp:], state_gdn_conv[l], state_gdn[l],
                             cache_swa_k[l].reshape(bs, WINDOW, SWA_KV_DIM),
                             cache_swa_v[l].reshape(bs, WINDOW, SWA_KV_DIM), wts,
                             bb=8, tt=ts, pbb=4, ptt=ts, cps=1, tq=ts, mask_history=False)
        outs_p.append(rest_p)
        outs_s.append(rest_s)
    stack = lambda outs, i: jnp.stack([o[i] for o in outs])
    return (yp, ys,
            stack(outs_p, 0), stack(outs_p, 1), stack(outs_p, 2), stack(outs_p, 3),
            stack(outs_s, 0), stack(outs_s, 1), stack(outs_s, 2), stack(outs_s, 3))
```

```python
import functools

import jax
import jax.numpy as jnp
from jax import lax
from jax.experimental import pallas as pl
from jax.experimental.pallas import tpu as pltpu

F32 = jnp.float32
BF16 = jnp.bfloat16

CHUNK = 64
GDN_HEADS = 8
GDN_DK = 128
GDN_DV = 128
GDN_QK_DIM = GDN_HEADS * GDN_DK
GDN_V_DIM = GDN_HEADS * GDN_DV
GDN_CONV_DIM = 2 * GDN_QK_DIM + GDN_V_DIM
CONV_W = 4
SWA_HQ = 16
SWA_HKV = 4
SWA_HD = 64
SWA_GROUP = SWA_HQ // SWA_HKV
SWA_Q_DIM = SWA_HQ * SWA_HD
SWA_KV_DIM = SWA_HKV * SWA_HD
WINDOW = 128
N_MOD = 9
EPS = 1e-6
LANES = 128
VMEM_LIMIT_BYTES = 60 * 1024 * 1024


def _dot(a, b):
    return jnp.dot(a, b, preferred_element_type=F32)


def _dot_nt(a, b):
    return lax.dot_general(a, b, (((1,), (1,)), ((), ())), preferred_element_type=F32)


def _dot_tn(a, b, precision=None):
    return lax.dot_general(a, b, (((0,), (0,)), ((), ())), preferred_element_type=F32,
                           precision=precision)


def _split_bf16(x):
    hi = x.astype(BF16)
    return hi, (x - hi.astype(F32)).astype(BF16)


def _silu(x):
    return x * jax.nn.sigmoid(x)


def _softplus(x):
    return jnp.maximum(x, 0.0) + jnp.log1p(jnp.exp(-jnp.abs(x)))


def _modulate(x, gain, shift, scale):
    y = x * lax.rsqrt(jnp.mean(x * x, axis=-1, keepdims=True) + EPS)
    return (y * gain) * (1 + scale) + shift


def _const_spec(shape):
    nd = len(shape)
    return pl.BlockSpec(shape, lambda *_: (0,) * nd, pipeline_mode=pl.Buffered(1))


def _params(semantics):
    return pltpu.CompilerParams(dimension_semantics=semantics, vmem_limit_bytes=VMEM_LIMIT_BYTES)


def _adaln_kernel(c_ref, w_ref, b_ref, o_ref):
    s = _silu(c_ref[...]).astype(BF16)
    o_ref[0] = _dot(s, w_ref[...].astype(BF16)) + b_ref[0]


def _adaln(c, w_ada, b_ada):
    nb, d = c.shape
    return pl.pallas_call(
        _adaln_kernel,
        out_shape=jax.ShapeDtypeStruct((N_MOD, nb, d), F32),
        grid=(N_MOD,),
        in_specs=[pl.BlockSpec((nb, d), lambda j: (0, 0)),
                  pl.BlockSpec((d, d), lambda j: (0, j)),
                  pl.BlockSpec((1, 1, d), lambda j: (j, 0, 0))],
        out_specs=pl.BlockSpec((1, nb, d), lambda j: (j, 0, 0)),
        compiler_params=_params(("parallel",)),
        name="adaln",
    )(c, w_ada, b_ada.reshape(N_MOD, 1, d))


def _ffn_body(x, mod_ref, gain_ref, win_ref, wout_ref, act_ref, tf):
    bb, tt, d = x.shape
    d_ff = wout_ref.shape[0]
    sh, sc, gt = mod_ref[0], mod_ref[1], mod_ref[2]
    h = _modulate(x, gain_ref[...], sh, sc).reshape(bb * tt, d).astype(BF16)
    for j in range(d_ff // tf):
        gu = _dot(h, win_ref[:, 2 * j * tf:2 * (j + 1) * tf])
        act_ref[:, j * tf:(j + 1) * tf] = (_silu(gu[:, :tf]) * gu[:, tf:]).astype(BF16)
    y = _dot(act_ref[...], wout_ref[...])
    return x + (0.5 * gt) * y.reshape(bb, tt, d)


def _ffn_kernel(x_ref, mod_ref, gain_ref, win_ref, wout_ref, o_ref, act_ref, *, tf):
    o_ref[...] = _ffn_body(x_ref[...], mod_ref, gain_ref, win_ref, wout_ref, act_ref, tf)


def _mix_ffn_kernel(x_ref, og_ref, z_ref, gnorm_ref, os_ref, gates_ref, bm_ref, wo_ref, gt2_ref,
                    mod_ref, gain_ref, win_ref, wout_ref, o_ref, act_ref, *, tf):
    x = x_ref[...]
    bb, tt, d = x.shape
    gnorm = gnorm_ref[...]
    og_heads = []
    for h in range(GDN_HEADS):
        o = og_ref[:, :, h * GDN_DV:(h + 1) * GDN_DV]
        on = o * lax.rsqrt(jnp.mean(o * o, axis=-1, keepdims=True) + EPS) * gnorm
        og_heads.append(on * _silu(z_ref[:, :, h * GDN_DV:(h + 1) * GDN_DV]))
    gl = jax.nn.sigmoid(gates_ref[...] + bm_ref[...])
    mix = gl * jnp.concatenate(og_heads + [os_ref[...]], axis=-1)
    y = _dot(mix.reshape(bb * tt, 2 * d).astype(BF16), wo_ref[...]).reshape(bb, tt, d)
    x1 = x + gt2_ref[0] * y
    o_ref[...] = _ffn_body(x1, mod_ref, gain_ref, win_ref, wout_ref, act_ref, tf)


def _tok_spec(bb, tt, width):
    return pl.BlockSpec((bb, tt, width), lambda b, t: (b, t, 0))


def _mod_spec(n, bb, d):
    return pl.BlockSpec((n, bb, 1, d), lambda b, t: (0, b, 0, 0))


def _ffn(x, mod3, gain, w_in_packed, w_out, *, bb, tt, tf):
    B, T, d = x.shape
    d_ff = w_out.shape[0]
    return pl.pallas_call(
        functools.partial(_ffn_kernel, tf=tf),
        out_shape=jax.ShapeDtypeStruct(x.shape, F32),
        grid=(B // bb, T // tt),
        in_specs=[_tok_spec(bb, tt, d), _mod_spec(3, bb, d), _const_spec((1, d)),
                  _const_spec((d, 2 * d_ff)), _const_spec((d_ff, d))],
        out_specs=_tok_spec(bb, tt, d),
        scratch_shapes=[pltpu.VMEM((bb * tt, d_ff), BF16)],
        compiler_params=_params(("parallel", "parallel")),
        name="ffn",
    )(x, mod3, gain, w_in_packed, w_out)


def _mix_ffn(x, og, z, gnorm, os_, gates, b_merge, w_o, mod_mix, mod3, gain, w_in_packed, w_out,
             *, bb, tt, tf):
    B, T, d = x.shape
    d_ff = w_out.shape[0]
    return pl.pallas_call(
        functools.partial(_mix_ffn_kernel, tf=tf),
        out_shape=jax.ShapeDtypeStruct(x.shape, F32),
        grid=(B // bb, T // tt),
        in_specs=[_tok_spec(bb, tt, d), _tok_spec(bb, tt, d), _tok_spec(bb, tt, d),
                  _const_spec((1, GDN_DV)), _tok_spec(bb, tt, d),
                  _tok_spec(bb, tt, 2 * d), _const_spec((1, 2 * d)), _const_spec((2 * d, d)),
                  pl.BlockSpec((1, bb, 1, d), lambda b, t: (2, b, 0, 0)),
                  _mod_spec(3, bb, d), _const_spec((1, d)),
                  _const_spec((d, 2 * d_ff)), _const_spec((d_ff, d))],
        out_specs=_tok_spec(bb, tt, d),
        scratch_shapes=[pltpu.VMEM((bb * tt, d_ff), BF16)],
        compiler_params=_params(("parallel", "parallel")),
        name="mix_ffn",
    )(x, og, z, gnorm, os_, gates, b_merge, w_o, mod_mix, mod3, gain, w_in_packed, w_out)


_PROJ_STEP = 512
_AB_WIDTH = 2 * LANES
_XP_PAD = 8


def _inproj_kernel(x_ref, mod_ref, gain_ref, w_ref, wab_ref, prefix_ref, convw_ref,
                   conv_ref, z_ref, q_ref, kv_ref, gates_ref, ab_ref, convout_ref, xp_scr):
    t_idx = pl.program_id(1)
    x = x_ref[...]
    bb, tt, d = x.shape
    hist = CONV_W - 1

    @pl.when(t_idx == 0)
    def _():
        xp_scr[:, _XP_PAD - hist:_XP_PAD, :] = prefix_ref[...]

    h = _modulate(x, gain_ref[...], mod_ref[0], mod_ref[1]).reshape(bb * tt, d).astype(BF16)
    cw = convw_ref[...]
    for lo in range(0, GDN_CONV_DIM, _PROJ_STEP):
        hi = lo + _PROJ_STEP
        pre = _dot(h, w_ref[:, lo:hi]).reshape(bb, tt, hi - lo)
        xp_scr[:, _XP_PAD:_XP_PAD + tt, lo:hi] = pre
        y = pre * cw[hist:hist + 1, lo:hi]
        for i in range(hist):
            y = y + xp_scr[:, _XP_PAD - hist + i:_XP_PAD - hist + i + tt, lo:hi] * cw[i:i + 1, lo:hi]
        conv_ref[:, :, lo:hi] = _silu(y)
    tail = xp_scr[:, tt:tt + _XP_PAD, :]
    xp_scr[:, 0:_XP_PAD, :] = tail

    col = GDN_CONV_DIM
    for ref in (z_ref, q_ref, kv_ref, gates_ref):
        width = ref.shape[-1]
        for lo in range(0, width, _PROJ_STEP):
            hi = min(lo + _PROJ_STEP, width)
            ref[:, :, lo:hi] = _dot(h, w_ref[:, col + lo:col + hi]).reshape(bb, tt, hi - lo)
        col += width
    ab_ref[...] = _dot(h, wab_ref[...]).reshape(bb, tt, _AB_WIDTH)

    @pl.when(t_idx == pl.num_programs(1) - 1)
    def _():
        convout_ref[...] = tail[:, _XP_PAD - hist:, :]


def _inproj(x, mod3, gain, w_main, w_ab, prefix, conv_w, *, bb, tt):
    B, T, d = x.shape
    widths = (GDN_CONV_DIM, GDN_V_DIM, SWA_Q_DIM, 2 * SWA_KV_DIM, 2 * d, _AB_WIDTH)
    hist_spec = pl.BlockSpec((bb, CONV_W - 1, GDN_CONV_DIM), lambda b, t: (b, 0, 0))
    return pl.pallas_call(
        _inproj_kernel,
        out_shape=[jax.ShapeDtypeStruct((B, T, w), F32) for w in widths]
        + [jax.ShapeDtypeStruct((B, CONV_W - 1, GDN_CONV_DIM), F32)],
        grid=(B // bb, T // tt),
        in_specs=[_tok_spec(bb, tt, d), _mod_spec(3, bb, d), _const_spec((1, d)),
                  _const_spec(w_main.shape), _const_spec(w_ab.shape), hist_spec,
                  _const_spec(conv_w.shape)],
        out_specs=[_tok_spec(bb, tt, w) for w in widths] + [hist_spec],
        scratch_shapes=[pltpu.VMEM((bb, _XP_PAD + tt, GDN_CONV_DIM), F32)],
        compiler_params=_params(("parallel", "arbitrary")),
        name="inproj",
    )(x, mod3, gain, w_main, w_ab, prefix, conv_w)


GDN_PACK = 4


def _pack_masks(size, pk):
    row = lax.broadcasted_iota(jnp.int32, (size, pk * size), 0)
    lane = lax.broadcasted_iota(jnp.int32, (size, pk * size), 1)
    col = lane % size
    merges = []
    s = 1
    while s < size:
        rb, cb = row // s, col // s
        merges.append((rb == cb + 1) & (rb % 2 == 1))
        s *= 2
    return dict(seg=lane // size, eye=row == col, incl=row >= col, strict=row > col, merges=merges)


def _select_seg(seg, parts):
    out = parts[-1]
    for k in range(len(parts) - 2, -1, -1):
        out = jnp.where(seg == k, parts[k], out)
    return out


def _block_diag_packed(m, seg, pk):
    zero = jnp.zeros_like(m)
    return jnp.concatenate([jnp.where(seg == k, m, zero) for k in range(pk)], axis=0)


def _block_diag(blocks):
    n = len(blocks)
    zero = jnp.zeros_like(blocks[0])
    return jnp.concatenate(
        [jnp.concatenate([blocks[k] if j == k else zero for j in range(n)], axis=1) for k in range(n)],
        axis=0)


def _gdn_kernel(conv_ref, ab_ref, s0_ref, alog_ref, dtb_ref, o_ref, sout_ref, s_scr, *, rb, cps):
    c = pl.program_id(1)
    L = CHUNK
    pk = GDN_PACK
    ngrp = GDN_HEADS // pk

    @pl.when(c == 0)
    def _():
        s_scr[...] = s0_ref[...]

    mk = _pack_masks(L, pk)
    seg, incl, strict, merges = mk["seg"], mk["incl"], mk["strict"], mk["merges"]
    row64 = lax.broadcasted_iota(jnp.int32, (L, L), 0)
    col64 = lax.broadcasted_iota(jnp.int32, (L, L), 1)
    incl64 = jnp.where(row64 >= col64, 1.0, 0.0).astype(BF16)
    incl3 = jnp.concatenate([incl64] * 3, axis=1)
    upper_p = jnp.where(strict, 0.0, 1.0).astype(BF16)
    upper3 = jnp.concatenate([upper_p] * 3, axis=0)

    chunks = [(r, i) for r in range(rb) for i in range(cps)]
    units = [(r, i, p) for (r, i) in chunks for p in range(ngrp)]
    heads = [(r, i, h) for (r, i) in chunks for h in range(GDN_HEADS)]

    def head_cols(off, r, i, h):
        return conv_ref[r, i * L:(i + 1) * L, off + h * GDN_DK:off + (h + 1) * GDN_DK]

    beta, gc, g2 = {}, {}, {}
    for (r, i) in chunks:
        ab = ab_ref[r, i * L:(i + 1) * L, :]
        g = -jnp.exp(alog_ref[...]) * _softplus(ab[:, :LANES] + dtb_ref[...])
        beta[r, i] = jax.nn.sigmoid(ab[:, LANES:])
        g_hi = g.astype(BF16)
        g_mid, g_lo = _split_bf16(g - g_hi.astype(F32))
        g3 = jnp.concatenate([g_hi, g_mid, g_lo], axis=0)
        gc[r, i] = _dot(incl3, g3)
        g2[r, i] = _dot_tn(g3, upper3)

    qn, kn, vb, qd, kdec, dl, gcb, btb = {}, {}, {}, {}, {}, {}, {}, {}
    for (r, i, h) in heads:
        qh, kh = head_cols(0, r, i, h), head_cols(GDN_QK_DIM, r, i, h)
        qn[r, i, h] = qh * lax.rsqrt(jnp.sum(qh * qh, axis=-1, keepdims=True) + EPS) * (GDN_DK ** -0.5)
        kn[r, i, h] = kh * lax.rsqrt(jnp.sum(kh * kh, axis=-1, keepdims=True) + EPS)
        gcb[r, i, h] = jnp.broadcast_to(gc[r, i][:, h:h + 1], (L, LANES))
        btb[r, i, h] = jnp.broadcast_to(beta[r, i][:, h:h + 1], (L, LANES))
        egb = jnp.exp(gcb[r, i, h])
        glb = gcb[r, i, h][L - 1:L, :]
        vb[r, i, h] = jnp.concatenate([head_cols(2 * GDN_QK_DIM, r, i, h) * btb[r, i, h],
                                       kn[r, i, h] * (btb[r, i, h] * egb)], axis=-1).astype(BF16)
        qd[r, i, h] = qn[r, i, h] * egb
        kdec[r, i, h] = (kn[r, i, h] * jnp.exp(glb - gcb[r, i, h])).astype(BF16)
        dl[r, i, h] = jnp.exp(glb)

    def packed(cols):
        reps = (pk * L) // LANES
        return _select_seg(seg, [jnp.concatenate([cc] * reps, axis=1) for cc in cols])

    gamma, n_p, qkg = {}, {}, {}
    for (r, i, p) in units:
        hs = range(pk * p, pk * (p + 1))
        gr_p = _select_seg(seg[0:1], [g2[r, i][h:h + 1, :] for h in hs])
        gamma[r, i, p] = jnp.exp(jnp.where(incl, packed([gcb[r, i, h] for h in hs]) - gr_p, -jnp.inf))
    for (r, i, p) in units:
        hs = range(pk * p, pk * (p + 1))
        kb = [kn[r, i, h].astype(BF16) for h in hs]
        lhs = jnp.concatenate([jnp.concatenate(kb, axis=1),
                               jnp.concatenate([qn[r, i, h].astype(BF16) for h in hs], axis=1)],
                              axis=0)
        res = _dot_nt(lhs, _block_diag(kb))
        bt_p = packed([btb[r, i, h] for h in hs])
        n_p[r, i, p] = jnp.where(strict, bt_p * res[:L] * gamma[r, i, p], 0.0)
        qkg[r, i, p] = (res[L:] * gamma[r, i, p]).astype(BF16)

    t = {u: jnp.where(mk["eye"], 1.0, 0.0) - jnp.where(merges[0], n_p[u], 0.0) for u in units}
    for m in merges[1:]:
        ct = {u: _dot(jnp.where(m, n_p[u], 0.0).astype(BF16), _block_diag_packed(t[u].astype(BF16), seg, pk))
              for u in units}
        t = {u: t[u] - _dot(t[u].astype(BF16), _block_diag_packed(ct[u].astype(BF16), seg, pk))
             for u in units}

    u_h, w_h = {}, {}
    for (r, i, p) in units:
        hs = range(pk * p, pk * (p + 1))
        uw = _dot(t[r, i, p].astype(BF16), _block_diag([vb[r, i, h] for h in hs]))
        for k, h in enumerate(hs):
            u_h[r, i, h] = uw[:, 2 * k * GDN_DV:(2 * k + 1) * GDN_DV]
            w_h[r, i, h] = uw[:, (2 * k + 1) * GDN_DV:(2 * k + 2) * GDN_DV]

    pairs = [(a, a + 1) for a in range(0, GDN_HEADS, 2)]
    wq = {}
    for (r, i) in chunks:
        for (a, b) in pairs:
            wq[r, i, a] = jnp.concatenate(
                [jnp.concatenate([w_h[r, i, a], w_h[r, i, b]], axis=1),
                 jnp.concatenate([qd[r, i, a], qd[r, i, b]], axis=1)],
                axis=0).astype(BF16)
    s_cur = {(r, h): s_scr[r, h] for r in range(rb) for h in range(GDN_HEADS)}
    for i in range(cps):
        rs, vn = {}, {}
        for r in range(rb):
            for (a, b) in pairs:
                rs[r, a] = _dot(wq[r, i, a], _block_diag([s_cur[r, a].astype(BF16), s_cur[r, b].astype(BF16)]))
        for r in range(rb):
            for (a, b) in pairs:
                vn[r, a] = (u_h[r, i, a] - rs[r, a][:L, :GDN_DV]).astype(BF16)
                vn[r, b] = (u_h[r, i, b] - rs[r, a][:L, GDN_DV:]).astype(BF16)
        for r in range(rb):
            for (a, b) in pairs:
                lo = (a % pk) * L
                o2 = _dot(qkg[r, i, a // pk][:, lo:lo + 2 * L], _block_diag([vn[r, a], vn[r, b]]))
                o_ref[r, i * L:(i + 1) * L, a * GDN_DV:(a + 2) * GDN_DV] = rs[r, a][L:, :] + o2
        for r in range(rb):
            for h in range(GDN_HEADS):
                s_cur[r, h] = dl[r, i, h] * s_cur[r, h] + _dot_tn(kdec[r, i, h], vn[r, h])

    for r in range(rb):
        for h in range(GDN_HEADS):
            s_scr[r, h] = s_cur[r, h]

    @pl.when(c == pl.num_programs(1) - 1)
    def _():
        sout_ref[...] = s_scr[...]


def _gdn(conv_act, ab, s0, alog_pad, dtb_pad, *, rb, cps):
    B, T, cd = conv_act.shape
    tg = cps * CHUNK
    state_spec = pl.BlockSpec((rb,) + s0.shape[1:], lambda b, c: (b, 0, 0, 0))
    return pl.pallas_call(
        functools.partial(_gdn_kernel, rb=rb, cps=cps),
        out_shape=[jax.ShapeDtypeStruct((B, T, GDN_V_DIM), F32),
                   jax.ShapeDtypeStruct(s0.shape, F32)],
        grid=(B // rb, T // tg),
        in_specs=[pl.BlockSpec((rb, tg, cd), lambda b, c: (b, c, 0)),
                  pl.BlockSpec((rb, tg, _AB_WIDTH), lambda b, c: (b, c, 0)),
                  state_spec, _const_spec((1, LANES)), _const_spec((1, LANES))],
        out_specs=[pl.BlockSpec((rb, tg, GDN_V_DIM), lambda b, c: (b, c, 0)), state_spec],
        scratch_shapes=[pltpu.VMEM((rb,) + s0.shape[1:], F32)],
        compiler_params=_params(("parallel", "arbitrary")),
        name="gdn",
    )(conv_act, ab, s0, alog_pad, dtb_pad)


def _segment_rms(x, gain, ones2):
    hi, lo = _split_bf16(x * x)
    ss = _dot(jnp.concatenate([hi, lo], axis=1), ones2)
    return x * lax.rsqrt(ss * (1.0 / SWA_HD) + EPS) * gain


def _swa_kernel(q_ref, kv_ref, kc_ref, vc_ref, qg_ref, kg_ref, sink_ref,
                o_ref, kout_ref, vout_ref, kwin, vwin, *, tq, mask_history):
    c = pl.program_id(1)
    L = CHUNK
    span = WINDOW + L
    kvd = SWA_KV_DIM

    @pl.when(c == 0)
    def _():
        kwin[0:WINDOW, :] = kc_ref[0]
        vwin[0:WINDOW, :] = vc_ref[0]

    seg_r = lax.broadcasted_iota(jnp.int32, (kvd, kvd), 0) // SWA_HD
    seg_c = lax.broadcasted_iota(jnp.int32, (kvd, kvd), 1) // SWA_HD
    ones_bd = jnp.where(seg_r == seg_c, 1.0, 0.0).astype(BF16)
    ones2 = jnp.concatenate([ones_bd, ones_bd], axis=0)

    kv = kv_ref[0]
    kwin[WINDOW:WINDOW + tq, :] = _segment_rms(kv[:, :kvd], kg_ref[...], ones2)
    vwin[WINDOW:WINDOW + tq, :] = kv[:, kvd:]

    kw, vw = kwin[...], vwin[...]
    seg_w = lax.broadcasted_iota(jnp.int32, kw.shape, 1) // SWA_HD
    k_only = [jnp.where(seg_w == hk, kw, 0.0).astype(BF16) for hk in range(SWA_HKV)]
    v_only = [jnp.where(seg_w == hk, vw, 0.0).astype(BF16) for hk in range(SWA_HKV)]

    rows = SWA_GROUP * L
    scale = SWA_HD ** -0.5
    qgain = qg_ref[...] * scale
    sink_col = [jnp.concatenate([jnp.broadcast_to(sink_ref[:, h:h + 1], (L, 1))
                                 for h in range(hk * SWA_GROUP, (hk + 1) * SWA_GROUP)], axis=0)
                for hk in range(SWA_HKV)]
    seg_o = lax.broadcasted_iota(jnp.int32, (rows, kvd), 1) // SWA_HD

    for j in range(tq // L):
        q_stack = jnp.concatenate([q_ref[0, j * L:(j + 1) * L, g * kvd:(g + 1) * kvd]
                                   for g in range(SWA_GROUP)], axis=0)
        qn = _segment_rms(q_stack, qgain, ones2).astype(BF16)
        if mask_history:
            kpos = c * tq + (j * L - WINDOW) + lax.broadcasted_iota(jnp.int32, (rows, span), 1)
            valid = kpos >= 0
        acc = None
        den_full = None
        for hk in range(SWA_HKV):
            s = _dot_nt(qn, k_only[hk][j * L:j * L + span])
            if mask_history:
                s = jnp.where(valid, s, -jnp.inf)
            m = jnp.maximum(jnp.max(s, axis=-1, keepdims=True), sink_col[hk])
            p = jnp.exp(s - m)
            den = jnp.sum(p, axis=-1, keepdims=True) + jnp.exp(sink_col[hk] - m)
            pv = _dot(p.astype(BF16), v_only[hk][j * L:j * L + span])
            acc = pv if acc is None else acc + pv
            den_full = den if den_full is None else jnp.where(seg_o == hk, den, den_full)
        o = acc / den_full
        for g in range(SWA_GROUP):
            o_ref[0, j * L:(j + 1) * L, g * kvd:(g + 1) * kvd] = o[g * L:(g + 1) * L]

    knew = kwin[tq:tq + WINDOW, :]
    vnew = vwin[tq:tq + WINDOW, :]
    kwin[0:WINDOW, :] = knew
    vwin[0:WINDOW, :] = vnew

    @pl.when(c == pl.num_programs(1) - 1)
    def _():
        kout_ref[0] = knew
        vout_ref[0] = vnew


def _swa(q, kv, k_cache, v_cache, q_gain, k_gain, sinks, *, tq, mask_history):
    B, T, _ = q.shape
    return pl.pallas_call(
        functools.partial(_swa_kernel, tq=tq, mask_history=mask_history),
        out_shape=[jax.ShapeDtypeStruct((B, T, SWA_Q_DIM), F32),
                   jax.ShapeDtypeStruct((B, WINDOW, SWA_KV_DIM), F32),
                   jax.ShapeDtypeStruct((B, WINDOW, SWA_KV_DIM), F32)],
        grid=(B, T // tq),
        in_specs=[pl.BlockSpec((1, tq, SWA_Q_DIM), lambda b, c: (b, c, 0)),
                  pl.BlockSpec((1, tq, 2 * SWA_KV_DIM), lambda b, c: (b, c, 0)),
                  pl.BlockSpec((1, WINDOW, SWA_KV_DIM), lambda b, c: (b, 0, 0)),
                  pl.BlockSpec((1, WINDOW, SWA_KV_DIM), lambda b, c: (b, 0, 0)),
                  _const_spec((1, SWA_KV_DIM)), _const_spec((1, SWA_KV_DIM)), _const_spec((1, SWA_HQ))],
        out_specs=[pl.BlockSpec((1, tq, SWA_Q_DIM), lambda b, c: (b, c, 0)),
                   pl.BlockSpec((1, WINDOW, SWA_KV_DIM), lambda b, c: (b, 0, 0)),
                   pl.BlockSpec((1, WINDOW, SWA_KV_DIM), lambda b, c: (b, 0, 0))],
        scratch_shapes=[pltpu.VMEM((WINDOW + tq, SWA_KV_DIM), F32),
                        pltpu.VMEM((WINDOW + tq, SWA_KV_DIM), F32)],
        compiler_params=_params(("parallel", "arbitrary")),
        name="swa",
    )(q, kv, k_cache, v_cache, q_gain, k_gain, sinks)


def _layer(x, mod, conv_prefix, s0, k_cache, v_cache, wts, *, bb, tt, pbb, ptt, rb, cps, tq, mask_history):
    (norm_ffn1, ffn1_in, ffn1_out, norm_mix, w_main, w_ab, conv_w, alog_pad, dtb_pad, gnorm,
     q_gain, k_gain, sinks, b_merge, w_o, norm_ffn2, ffn2_in, ffn2_out, tf) = wts
    B, T, d = x.shape
    x = _ffn(x, mod[0:3], norm_ffn1, ffn1_in, ffn1_out, bb=bb, tt=tt, tf=tf)
    conv_act, z, q_s, kv_s, gates, ab, new_conv = _inproj(x, mod[3:6], norm_mix, w_main, w_ab,
                                                          conv_prefix, conv_w, bb=pbb, tt=ptt)
    o_g, new_s = _gdn(conv_act, ab, s0, alog_pad, dtb_pad, rb=rb, cps=cps)
    o_s, new_k, new_v = _swa(q_s, kv_s, k_cache, v_cache, q_gain, k_gain, sinks,
                             tq=tq, mask_history=mask_history)
    x = _mix_ffn(x, o_g, z, gnorm, o_s, gates, b_merge, w_o, mod[3:6], mod[6:9], norm_ffn2, ffn2_in,
                 ffn2_out, bb=bb, tt=tt, tf=tf)
    new_k = new_k.reshape(B, WINDOW, SWA_HKV, SWA_HD)
    new_v = new_v.reshape(B, WINDOW, SWA_HKV, SWA_HD)
    return x, new_conv, new_s, new_k, new_v


def _pack_gate_up(w_in, tf):
    d, two_f = w_in.shape
    f = two_f // 2
    g = w_in[:, :f].reshape(d, f // tf, tf)
    u = w_in[:, f:].reshape(d, f // tf, tf)
    return jnp.concatenate([g, u], axis=-1).reshape(d, two_f).astype(BF16)


def _group_major(w, axis):
    shape = w.shape
    split = shape[:axis] + (SWA_HKV, SWA_GROUP, SWA_HD) + shape[axis + 1:]
    return jnp.swapaxes(w.reshape(split), axis, axis + 1).reshape(shape)


def _pad_lanes(v):
    return jnp.pad(v.astype(F32), (0, LANES - v.shape[0])).reshape(1, LANES)


def kernel(x_prompt, x_sample, state_gdn_conv, state_gdn, cache_swa_k, cache_swa_v, c_prompt, c_sample, w_ada, b_ada, norm_ffn1, ffn1_w_in, ffn1_w_out, norm_mix, w_in, gdn_conv_w, gdn_a_log, gdn_dt_bias, gdn_norm, swa_q_norm, swa_k_norm, swa_sinks, b_merge, w_out, norm_ffn2, ffn2_w_in, ffn2_w_out):
    depth = w_ada.shape[0]
    bp, tp, d = x_prompt.shape
    bs, ts, _ = x_sample.shape
    tf = 256
    yp, ys = x_prompt, x_sample
    outs_p, outs_s = [], []
    for l in range(depth):
        mod = _adaln(jnp.concatenate([c_prompt, c_sample], axis=0), w_ada[l], b_ada[l])
        mod = mod.reshape(N_MOD, bp + bs, 1, d)
        wl = w_in[l]
        o_conv, o_z = GDN_CONV_DIM, GDN_CONV_DIM + GDN_V_DIM
        o_a, o_b = o_z, o_z + GDN_HEADS
        o_q = o_b + GDN_HEADS
        o_k = o_q + SWA_Q_DIM
        o_g = o_k + 2 * SWA_KV_DIM
        o_gb = o_g + d
        w_main = jnp.concatenate(
            [wl[:, :o_z], _group_major(wl[:, o_q:o_k], 1), wl[:, o_k:o_gb], _group_major(wl[:, o_gb:], 1)],
            axis=1).astype(BF16)
        pad = jnp.zeros((d, LANES - GDN_HEADS), wl.dtype)
        w_ab = jnp.concatenate([wl[:, o_a:o_b], pad, wl[:, o_b:o_q], pad], axis=1).astype(BF16)
        bm = jnp.concatenate([b_merge[l][:d], _group_major(b_merge[l][d:], 0)]).reshape(1, 2 * d)
        w_o2 = jnp.concatenate([w_out[l], _group_major(w_out[l], 0)], axis=0).astype(BF16)
        wts = (norm_ffn1[l].reshape(1, d), _pack_gate_up(ffn1_w_in[l], tf), ffn1_w_out[l].astype(BF16),
               norm_mix[l].reshape(1, d), w_main, w_ab, gdn_conv_w[l],
               _pad_lanes(gdn_a_log[l]), _pad_lanes(gdn_dt_bias[l]), gdn_norm[l].reshape(1, GDN_DV),
               jnp.tile(swa_q_norm[l], SWA_HKV).reshape(1, SWA_KV_DIM),
               jnp.tile(swa_k_norm[l], SWA_HKV).reshape(1, SWA_KV_DIM),
               swa_sinks[l].reshape(1, SWA_HQ).astype(F32), bm,
               w_o2, norm_ffn2[l].reshape(1, d),
               _pack_gate_up(ffn2_w_in[l], tf), ffn2_w_out[l].astype(BF16), tf)
        zero_conv = jnp.zeros((bp, CONV_W - 1, GDN_CONV_DIM), F32)
        zero_s = jnp.zeros((bp, GDN_HEADS, GDN_DK, GDN_DV), F32)
        zero_kv = jnp.zeros((bp, WINDOW, SWA_KV_DIM), F32)
        yp, *rest_p = _layer(yp, mod[:, :bp], zero_conv, zero_s, zero_kv, zero_kv, wts,
                             bb=1, tt=512, pbb=1, ptt=256, rb=2, cps=4, tq=256, mask_history=True)
        ys, *rest_s = _layer(ys, mod[:, bp:], state_gdn_conv[l], state_gdn[l],
                             cache_swa_k[l].reshape(bs, WINDOW, SWA_KV_DIM),
                             cache_swa_v[l].reshape(bs, WINDOW, SWA_KV_DIM), wts,
                             bb=8, tt=ts, pbb=4, ptt=ts, rb=4, cps=1, tq=ts, mask_history=False)
        outs_p.append(rest_p)
        outs_s.append(rest_s)
    stack = lambda outs, i: jnp.stack([o[i] for o in outs])
    return (yp, ys,
            stack(outs_p, 0), stack(outs_p, 1), stack(outs_p, 2), stack(outs_p, 3),
            stack(outs_s, 0), stack(outs_s, 1), stack(outs_s, 2), stack(outs_s, 3))
```

```python
import functools

import jax
import jax.numpy as jnp
from jax import lax
from jax.experimental import pallas as pl
from jax.experimental.pallas import tpu as pltpu

F32 = jnp.float32
BF16 = jnp.bfloat16

CHUNK = 64
GDN_HEADS = 8
GDN_DK = 128
GDN_DV = 128
GDN_QK_DIM = GDN_HEADS * GDN_DK
GDN_V_DIM = GDN_HEADS * GDN_DV
GDN_CONV_DIM = 2 * GDN_QK_DIM + GDN_V_DIM
CONV_W = 4
SWA_HQ = 16
SWA_HKV = 4
SWA_HD = 64
SWA_GROUP = SWA_HQ // SWA_HKV
SWA_Q_DIM = SWA_HQ * SWA_HD
SWA_KV_DIM = SWA_HKV * SWA_HD
WINDOW = 128
N_MOD = 9
EPS = 1e-6
LANES = 128
VMEM_LIMIT_BYTES = 60 * 1024 * 1024


def _dot(a, b):
    return jnp.dot(a, b, preferred_element_type=F32)


def _dot_nt(a, b):
    return lax.dot_general(a, b, (((1,), (1,)), ((), ())), preferred_element_type=F32)


def _dot_tn(a, b, precision=None):
    return lax.dot_general(a, b, (((0,), (0,)), ((), ())), preferred_element_type=F32,
                           precision=precision)


def _split_bf16(x):
    hi = x.astype(BF16)
    return hi, (x - hi.astype(F32)).astype(BF16)


def _silu(x):
    return x * jax.nn.sigmoid(x)


def _softplus(x):
    return jnp.maximum(x, 0.0) + jnp.log1p(jnp.exp(-jnp.abs(x)))


def _modulate(x, gain, shift, scale):
    y = x * lax.rsqrt(jnp.mean(x * x, axis=-1, keepdims=True) + EPS)
    return (y * gain) * (1 + scale) + shift


def _const_spec(shape):
    nd = len(shape)
    return pl.BlockSpec(shape, lambda *_: (0,) * nd, pipeline_mode=pl.Buffered(1))


def _params(semantics):
    return pltpu.CompilerParams(dimension_semantics=semantics, vmem_limit_bytes=VMEM_LIMIT_BYTES)


def _adaln_kernel(c_ref, w_ref, b_ref, o_ref):
    s = _silu(c_ref[...]).astype(BF16)
    o_ref[0] = _dot(s, w_ref[...].astype(BF16)) + b_ref[0]


def _adaln(c, w_ada, b_ada):
    nb, d = c.shape
    return pl.pallas_call(
        _adaln_kernel,
        out_shape=jax.ShapeDtypeStruct((N_MOD, nb, d), F32),
        grid=(N_MOD,),
        in_specs=[pl.BlockSpec((nb, d), lambda j: (0, 0)),
                  pl.BlockSpec((d, d), lambda j: (0, j)),
                  pl.BlockSpec((1, 1, d), lambda j: (j, 0, 0))],
        out_specs=pl.BlockSpec((1, nb, d), lambda j: (j, 0, 0)),
        compiler_params=_params(("parallel",)),
        name="adaln",
    )(c, w_ada, b_ada.reshape(N_MOD, 1, d))


def _ffn_body(x, mod_ref, gain_ref, win_ref, wout_ref, act_ref, tf):
    bb, tt, d = x.shape
    d_ff = wout_ref.shape[0]
    sh, sc, gt = mod_ref[0], mod_ref[1], mod_ref[2]
    h = _modulate(x, gain_ref[...], sh, sc).reshape(bb * tt, d).astype(BF16)
    for j in range(d_ff // tf):
        gu = _dot(h, win_ref[:, 2 * j * tf:2 * (j + 1) * tf])
        act_ref[:, j * tf:(j + 1) * tf] = (_silu(gu[:, :tf]) * gu[:, tf:]).astype(BF16)
    y = _dot(act_ref[...], wout_ref[...])
    return x + (0.5 * gt) * y.reshape(bb, tt, d)


def _ffn_kernel(x_ref, mod_ref, gain_ref, win_ref, wout_ref, o_ref, act_ref, *, tf):
    o_ref[...] = _ffn_body(x_ref[...], mod_ref, gain_ref, win_ref, wout_ref, act_ref, tf)


def _mix_ffn_kernel(x_ref, og_ref, z_ref, gnorm_ref, os_ref, gates_ref, bm_ref, wo_ref, gt2_ref,
                    mod_ref, gain_ref, win_ref, wout_ref, o_ref, act_ref, *, tf):
    x = x_ref[...]
    bb, tt, d = x.shape
    gnorm = gnorm_ref[...]
    og_heads = []
    for h in range(GDN_HEADS):
        o = og_ref[:, :, h * GDN_DV:(h + 1) * GDN_DV]
        on = o * lax.rsqrt(jnp.mean(o * o, axis=-1, keepdims=True) + EPS) * gnorm
        og_heads.append(on * _silu(z_ref[:, :, h * GDN_DV:(h + 1) * GDN_DV]))
    gl = jax.nn.sigmoid(gates_ref[...] + bm_ref[...])
    mix = gl * jnp.concatenate(og_heads + [os_ref[...]], axis=-1)
    y = _dot(mix.reshape(bb * tt, 2 * d).astype(BF16), wo_ref[...]).reshape(bb, tt, d)
    x1 = x + gt2_ref[0] * y
    o_ref[...] = _ffn_body(x1, mod_ref, gain_ref, win_ref, wout_ref, act_ref, tf)


def _tok_spec(bb, tt, width):
    return pl.BlockSpec((bb, tt, width), lambda b, t: (b, t, 0))


def _mod_spec(n, bb, d):
    return pl.BlockSpec((n, bb, 1, d), lambda b, t: (0, b, 0, 0))


def _ffn(x, mod3, gain, w_in_packed, w_out, *, bb, tt, tf):
    B, T, d = x.shape
    d_ff = w_out.shape[0]
    return pl.pallas_call(
        functools.partial(_ffn_kernel, tf=tf),
        out_shape=jax.ShapeDtypeStruct(x.shape, F32),
        grid=(B // bb, T // tt),
        in_specs=[_tok_spec(bb, tt, d), _mod_spec(3, bb, d), _const_spec((1, d)),
                  _const_spec((d, 2 * d_ff)), _const_spec((d_ff, d))],
        out_specs=_tok_spec(bb, tt, d),
        scratch_shapes=[pltpu.VMEM((bb * tt, d_ff), BF16)],
        compiler_params=_params(("parallel", "parallel")),
        name="ffn",
    )(x, mod3, gain, w_in_packed, w_out)


def _mix_ffn(x, og, z, gnorm, os_, gates, b_merge, w_o, mod_mix, mod3, gain, w_in_packed, w_out,
             *, bb, tt, tf):
    B, T, d = x.shape
    d_ff = w_out.shape[0]
    return pl.pallas_call(
        functools.partial(_mix_ffn_kernel, tf=tf),
        out_shape=jax.ShapeDtypeStruct(x.shape, F32),
        grid=(B // bb, T // tt),
        in_specs=[_tok_spec(bb, tt, d), _tok_spec(bb, tt, d), _tok_spec(bb, tt, d),
                  _const_spec((1, GDN_DV)), _tok_spec(bb, tt, d),
                  _tok_spec(bb, tt, 2 * d), _const_spec((1, 2 * d)), _const_spec((2 * d, d)),
                  pl.BlockSpec((1, bb, 1, d), lambda b, t: (2, b, 0, 0)),
                  _mod_spec(3, bb, d), _const_spec((1, d)),
                  _const_spec((d, 2 * d_ff)), _const_spec((d_ff, d))],
        out_specs=_tok_spec(bb, tt, d),
        scratch_shapes=[pltpu.VMEM((bb * tt, d_ff), BF16)],
        compiler_params=_params(("parallel", "parallel")),
        name="mix_ffn",
    )(x, og, z, gnorm, os_, gates, b_merge, w_o, mod_mix, mod3, gain, w_in_packed, w_out)


_PROJ_STEP = 512
_AB_WIDTH = 2 * LANES
_XP_PAD = 8


def _inproj_kernel(x_ref, mod_ref, gain_ref, w_ref, wab_ref, prefix_ref, convw_ref,
                   conv_ref, z_ref, q_ref, kv_ref, gates_ref, ab_ref, convout_ref, xp_scr):
    t_idx = pl.program_id(1)
    x = x_ref[...]
    bb, tt, d = x.shape
    hist = CONV_W - 1

    @pl.when(t_idx == 0)
    def _():
        xp_scr[:, _XP_PAD - hist:_XP_PAD, :] = prefix_ref[...]

    h = _modulate(x, gain_ref[...], mod_ref[0], mod_ref[1]).reshape(bb * tt, d).astype(BF16)
    cw = convw_ref[...]
    for lo in range(0, GDN_CONV_DIM, _PROJ_STEP):
        hi = lo + _PROJ_STEP
        pre = _dot(h, w_ref[:, lo:hi]).reshape(bb, tt, hi - lo)
        xp_scr[:, _XP_PAD:_XP_PAD + tt, lo:hi] = pre
        y = pre * cw[hist:hist + 1, lo:hi]
        for i in range(hist):
            y = y + xp_scr[:, _XP_PAD - hist + i:_XP_PAD - hist + i + tt, lo:hi] * cw[i:i + 1, lo:hi]
        conv_ref[:, :, lo:hi] = _silu(y)
    tail = xp_scr[:, tt:tt + _XP_PAD, :]
    xp_scr[:, 0:_XP_PAD, :] = tail

    col = GDN_CONV_DIM
    for ref in (z_ref, q_ref, kv_ref, gates_ref):
        width = ref.shape[-1]
        for lo in range(0, width, _PROJ_STEP):
            hi = min(lo + _PROJ_STEP, width)
            ref[:, :, lo:hi] = _dot(h, w_ref[:, col + lo:col + hi]).reshape(bb, tt, hi - lo)
        col += width
    ab_ref[...] = _dot(h, wab_ref[...]).reshape(bb, tt, _AB_WIDTH)

    @pl.when(t_idx == pl.num_programs(1) - 1)
    def _():
        convout_ref[...] = tail[:, _XP_PAD - hist:, :]


def _inproj(x, mod3, gain, w_main, w_ab, prefix, conv_w, *, bb, tt):
    B, T, d = x.shape
    widths = (GDN_CONV_DIM, GDN_V_DIM, SWA_Q_DIM, 2 * SWA_KV_DIM, 2 * d, _AB_WIDTH)
    hist_spec = pl.BlockSpec((bb, CONV_W - 1, GDN_CONV_DIM), lambda b, t: (b, 0, 0))
    return pl.pallas_call(
        _inproj_kernel,
        out_shape=[jax.ShapeDtypeStruct((B, T, w), F32) for w in widths]
        + [jax.ShapeDtypeStruct((B, CONV_W - 1, GDN_CONV_DIM), F32)],
        grid=(B // bb, T // tt),
        in_specs=[_tok_spec(bb, tt, d), _mod_spec(3, bb, d), _const_spec((1, d)),
                  _const_spec(w_main.shape), _const_spec(w_ab.shape), hist_spec,
                  _const_spec(conv_w.shape)],
        out_specs=[_tok_spec(bb, tt, w) for w in widths] + [hist_spec],
        scratch_shapes=[pltpu.VMEM((bb, _XP_PAD + tt, GDN_CONV_DIM), F32)],
        compiler_params=_params(("parallel", "arbitrary")),
        name="inproj",
    )(x, mod3, gain, w_main, w_ab, prefix, conv_w)


GDN_PACK = 4


def _pack_masks(size, pk):
    row = lax.broadcasted_iota(jnp.int32, (size, pk * size), 0)
    lane = lax.broadcasted_iota(jnp.int32, (size, pk * size), 1)
    col = lane % size
    merges = []
    s = 1
    while s < size:
        rb, cb = row // s, col // s
        merges.append((rb == cb + 1) & (rb % 2 == 1))
        s *= 2
    return dict(seg=lane // size, eye=row == col, incl=row >= col, strict=row > col, merges=merges)


def _select_seg(seg, parts):
    out = parts[-1]
    for k in range(len(parts) - 2, -1, -1):
        out = jnp.where(seg == k, parts[k], out)
    return out


def _block_diag_packed(m, seg, pk):
    zero = jnp.zeros_like(m)
    return jnp.concatenate([jnp.where(seg == k, m, zero) for k in range(pk)], axis=0)


def _block_diag(blocks):
    n = len(blocks)
    zero = jnp.zeros_like(blocks[0])
    return jnp.concatenate(
        [jnp.concatenate([blocks[k] if j == k else zero for j in range(n)], axis=1) for k in range(n)],
        axis=0)


def _gdn_kernel(conv_ref, ab_ref, s0_ref, alog_ref, dtb_ref, o_ref, sout_ref, s_scr, *, rb, cps):
    c = pl.program_id(1)
    L = CHUNK
    pk = GDN_PACK
    ngrp = GDN_HEADS // pk

    @pl.when(c == 0)
    def _():
        s_scr[...] = s0_ref[...]

    mk = _pack_masks(L, pk)
    seg, incl, strict, merges = mk["seg"], mk["incl"], mk["strict"], mk["merges"]
    row64 = lax.broadcasted_iota(jnp.int32, (L, L), 0)
    col64 = lax.broadcasted_iota(jnp.int32, (L, L), 1)
    incl64 = jnp.where(row64 >= col64, 1.0, 0.0).astype(BF16)
    incl3 = jnp.concatenate([incl64] * 3, axis=1)
    upper_p = jnp.where(strict, 0.0, 1.0).astype(BF16)
    upper3 = jnp.concatenate([upper_p] * 3, axis=0)

    chunks = [(r, i) for r in range(rb) for i in range(cps)]
    units = [(r, i, p) for (r, i) in chunks for p in range(ngrp)]
    heads = [(r, i, h) for (r, i) in chunks for h in range(GDN_HEADS)]

    def head_cols(off, r, i, h):
        return conv_ref[r, i * L:(i + 1) * L, off + h * GDN_DK:off + (h + 1) * GDN_DK]

    beta, gc, g2 = {}, {}, {}
    for (r, i) in chunks:
        ab = ab_ref[r, i * L:(i + 1) * L, :]
        g = -jnp.exp(alog_ref[...]) * _softplus(ab[:, :LANES] + dtb_ref[...])
        beta[r, i] = jax.nn.sigmoid(ab[:, LANES:])
        g_hi = g.astype(BF16)
        g_mid, g_lo = _split_bf16(g - g_hi.astype(F32))
        g3 = jnp.concatenate([g_hi, g_mid, g_lo], axis=0)
        gc[r, i] = _dot(incl3, g3)
        g2[r, i] = _dot_tn(g3, upper3)

    qn, kn, vb, qd, kdec, dl, gcb, btb = {}, {}, {}, {}, {}, {}, {}, {}
    for (r, i, h) in heads:
        qh, kh = head_cols(0, r, i, h), head_cols(GDN_QK_DIM, r, i, h)
        qn[r, i, h] = qh * lax.rsqrt(jnp.sum(qh * qh, axis=-1, keepdims=True) + EPS) * (GDN_DK ** -0.5)
        kn[r, i, h] = kh * lax.rsqrt(jnp.sum(kh * kh, axis=-1, keepdims=True) + EPS)
        gcb[r, i, h] = jnp.broadcast_to(gc[r, i][:, h:h + 1], (L, LANES))
        btb[r, i, h] = jnp.broadcast_to(beta[r, i][:, h:h + 1], (L, LANES))
        egb = jnp.exp(gcb[r, i, h])
        glb = gcb[r, i, h][L - 1:L, :]
        vb[r, i, h] = jnp.concatenate([head_cols(2 * GDN_QK_DIM, r, i, h) * btb[r, i, h],
                                       kn[r, i, h] * (btb[r, i, h] * egb)], axis=-1).astype(BF16)
        qd[r, i, h] = qn[r, i, h] * egb
        kdec[r, i, h] = (kn[r, i, h] * jnp.exp(glb - gcb[r, i, h])).astype(BF16)
        dl[r, i, h] = jnp.exp(glb)

    def packed(cols):
        reps = (pk * L) // LANES
        return _select_seg(seg, [jnp.concatenate([cc] * reps, axis=1) for cc in cols])

    gamma, n_p, qkg = {}, {}, {}
    for (r, i, p) in units:
        hs = range(pk * p, pk * (p + 1))
        gr_p = _select_seg(seg[0:1], [g2[r, i][h:h + 1, :] for h in hs])
        gamma[r, i, p] = jnp.exp(jnp.where(incl, packed([gcb[r, i, h] for h in hs]) - gr_p, -jnp.inf))
    for (r, i, p) in units:
        hs = range(pk * p, pk * (p + 1))
        kb = [kn[r, i, h].astype(BF16) for h in hs]
        lhs = jnp.concatenate([jnp.concatenate(kb, axis=1),
                               jnp.concatenate([qn[r, i, h].astype(BF16) for h in hs], axis=1)],
                              axis=0)
        res = _dot_nt(lhs, _block_diag(kb))
        bt_p = packed([btb[r, i, h] for h in hs])
        n_p[r, i, p] = jnp.where(strict, bt_p * res[:L] * gamma[r, i, p], 0.0)
        qkg[r, i, p] = (res[L:] * gamma[r, i, p]).astype(BF16)

    t = {u: jnp.where(mk["eye"], 1.0, 0.0) - jnp.where(merges[0], n_p[u], 0.0) for u in units}
    for m in merges[1:]:
        ct = {u: _dot(jnp.where(m, n_p[u], 0.0).astype(BF16), _block_diag_packed(t[u].astype(BF16), seg, pk))
              for u in units}
        t = {u: t[u] - _dot(t[u].astype(BF16), _block_diag_packed(ct[u].astype(BF16), seg, pk))
             for u in units}

    u_h, w_h = {}, {}
    for (r, i, p) in units:
        hs = range(pk * p, pk * (p + 1))
        uw = _dot(t[r, i, p].astype(BF16), _block_diag([vb[r, i, h] for h in hs]))
        for k, h in enumerate(hs):
            u_h[r, i, h] = uw[:, 2 * k * GDN_DV:(2 * k + 1) * GDN_DV]
            w_h[r, i, h] = uw[:, (2 * k + 1) * GDN_DV:(2 * k + 2) * GDN_DV]

    pairs = [(a, a + 1) for a in range(0, GDN_HEADS, 2)]
    wq = {}
    for (r, i) in chunks:
        for (a, b) in pairs:
            wq[r, i, a] = jnp.concatenate(
                [jnp.concatenate([w_h[r, i, a], w_h[r, i, b]], axis=1),
                 jnp.concatenate([qd[r, i, a], qd[r, i, b]], axis=1)],
                axis=0).astype(BF16)
    s_cur = {(r, h): s_scr[r, h] for r in range(rb) for h in range(GDN_HEADS)}
    for i in range(cps):
        rs, vn = {}, {}
        for r in range(rb):
            for (a, b) in pairs:
                rs[r, a] = _dot(wq[r, i, a], _block_diag([s_cur[r, a].astype(BF16), s_cur[r, b].astype(BF16)]))
        for r in range(rb):
            for (a, b) in pairs:
                vn[r, a] = (u_h[r, i, a] - rs[r, a][:L, :GDN_DV]).astype(BF16)
                vn[r, b] = (u_h[r, i, b] - rs[r, a][:L, GDN_DV:]).astype(BF16)
        for r in range(rb):
            for (a, b) in pairs:
                lo = (a % pk) * L
                o2 = _dot(qkg[r, i, a // pk][:, lo:lo + 2 * L], _block_diag([vn[r, a], vn[r, b]]))
                o_ref[r, i * L:(i + 1) * L, a * GDN_DV:(a + 2) * GDN_DV] = rs[r, a][L:, :] + o2
        for r in range(rb):
            for h in range(GDN_HEADS):
                s_cur[r, h] = dl[r, i, h] * s_cur[r, h] + _dot_tn(kdec[r, i, h], vn[r, h])

    for r in range(rb):
        for h in range(GDN_HEADS):
            s_scr[r, h] = s_cur[r, h]

    @pl.when(c == pl.num_programs(1) - 1)
    def _():
        sout_ref[...] = s_scr[...]


def _gdn(conv_act, ab, s0, alog_pad, dtb_pad, *, rb, cps):
    B, T, cd = conv_act.shape
    tg = cps * CHUNK
    state_spec = pl.BlockSpec((rb,) + s0.shape[1:], lambda b, c: (b, 0, 0, 0))
    return pl.pallas_call(
        functools.partial(_gdn_kernel, rb=rb, cps=cps),
        out_shape=[jax.ShapeDtypeStruct((B, T, GDN_V_DIM), F32),
                   jax.ShapeDtypeStruct(s0.shape, F32)],
        grid=(B // rb, T // tg),
        in_specs=[pl.BlockSpec((rb, tg, cd), lambda b, c: (b, c, 0)),
                  pl.BlockSpec((rb, tg, _AB_WIDTH), lambda b, c: (b, c, 0)),
                  state_spec, _const_spec((1, LANES)), _const_spec((1, LANES))],
        out_specs=[pl.BlockSpec((rb, tg, GDN_V_DIM), lambda b, c: (b, c, 0)), state_spec],
        scratch_shapes=[pltpu.VMEM((rb,) + s0.shape[1:], F32)],
        compiler_params=_params(("parallel", "arbitrary")),
        name="gdn",
    )(conv_act, ab, s0, alog_pad, dtb_pad)


def _segment_rms(x, gain, ones2):
    hi, lo = _split_bf16(x * x)
    ss = _dot(jnp.concatenate([hi, lo], axis=1), ones2)
    return x * lax.rsqrt(ss * (1.0 / SWA_HD) + EPS) * gain


def _swa_kernel(q_ref, kv_ref, kc_ref, vc_ref, qg_ref, kg_ref, sink_ref,
                o_ref, kout_ref, vout_ref, kwin, vwin, *, tq, mask_history):
    c = pl.program_id(1)
    L = CHUNK
    span = WINDOW + L
    kvd = SWA_KV_DIM

    @pl.when(c == 0)
    def _():
        kwin[0:WINDOW, :] = kc_ref[0]
        vwin[0:WINDOW, :] = vc_ref[0]

    seg_r = lax.broadcasted_iota(jnp.int32, (kvd, kvd), 0) // SWA_HD
    seg_c = lax.broadcasted_iota(jnp.int32, (kvd, kvd), 1) // SWA_HD
    ones_bd = jnp.where(seg_r == seg_c, 1.0, 0.0).astype(BF16)
    ones2 = jnp.concatenate([ones_bd, ones_bd], axis=0)

    kv = kv_ref[0]
    kwin[WINDOW:WINDOW + tq, :] = _segment_rms(kv[:, :kvd], kg_ref[...], ones2)
    vwin[WINDOW:WINDOW + tq, :] = kv[:, kvd:]

    kw, vw = kwin[...], vwin[...]
    seg_w = lax.broadcasted_iota(jnp.int32, kw.shape, 1) // SWA_HD
    k_only = [jnp.where(seg_w == hk, kw, 0.0).astype(BF16) for hk in range(SWA_HKV)]
    v_only = [jnp.where(seg_w == hk, vw, 0.0).astype(BF16) for hk in range(SWA_HKV)]

    rows = SWA_GROUP * L
    scale = SWA_HD ** -0.5
    qgain = qg_ref[...] * scale
    sink_col = [jnp.concatenate([jnp.broadcast_to(sink_ref[:, h:h + 1], (L, 1))
                                 for h in range(hk * SWA_GROUP, (hk + 1) * SWA_GROUP)], axis=0)
                for hk in range(SWA_HKV)]
    seg_o = lax.broadcasted_iota(jnp.int32, (rows, kvd), 1) // SWA_HD

    nj = tq // L
    units = [(j, hk) for j in range(nj) for hk in range(SWA_HKV)]
    qn = [None] * nj
    for j in range(nj):
        q_stack = jnp.concatenate([q_ref[0, j * L:(j + 1) * L, g * kvd:(g + 1) * kvd]
                                   for g in range(SWA_GROUP)], axis=0)
        qn[j] = _segment_rms(q_stack, qgain, ones2).astype(BF16)

    def scores(j, hk):
        return _dot_nt(qn[j], k_only[hk][j * L:j * L + span])

    sink_all = jnp.concatenate(sink_col, axis=0)
    s_next = [scores(0, hk) for hk in range(SWA_HKV)]
    for j in range(nj):
        s_all = jnp.concatenate(s_next, axis=0)
        if j + 1 < nj:
            s_next = [scores(j + 1, hk) for hk in range(SWA_HKV)]
        if mask_history:
            kpos = c * tq + (j * L - WINDOW) + lax.broadcasted_iota(jnp.int32, s_all.shape, 1)
            s_all = jnp.where(kpos >= 0, s_all, -jnp.inf)
        m = jnp.maximum(jnp.max(s_all, axis=-1, keepdims=True), sink_all)
        p = jnp.exp(s_all - m)
        den = jnp.sum(p, axis=-1, keepdims=True) + jnp.exp(sink_all - m)
        pb = p.astype(BF16)
        acc = None
        den_full = None
        for hk in range(SWA_HKV):
            pv = _dot(pb[hk * rows:(hk + 1) * rows], v_only[hk][j * L:j * L + span])
            acc = pv if acc is None else acc + pv
            dh = den[hk * rows:(hk + 1) * rows]
            den_full = dh if den_full is None else jnp.where(seg_o == hk, dh, den_full)
        o = acc / den_full
        for g in range(SWA_GROUP):
            o_ref[0, j * L:(j + 1) * L, g * kvd:(g + 1) * kvd] = o[g * L:(g + 1) * L]

    knew = kwin[tq:tq + WINDOW, :]
    vnew = vwin[tq:tq + WINDOW, :]
    kwin[0:WINDOW, :] = knew
    vwin[0:WINDOW, :] = vnew

    @pl.when(c == pl.num_programs(1) - 1)
    def _():
        kout_ref[0] = knew
        vout_ref[0] = vnew


def _swa(q, kv, k_cache, v_cache, q_gain, k_gain, sinks, *, tq, mask_history):
    B, T, _ = q.shape
    return pl.pallas_call(
        functools.partial(_swa_kernel, tq=tq, mask_history=mask_history),
        out_shape=[jax.ShapeDtypeStruct((B, T, SWA_Q_DIM), F32),
                   jax.ShapeDtypeStruct((B, WINDOW, SWA_KV_DIM), F32),
                   jax.ShapeDtypeStruct((B, WINDOW, SWA_KV_DIM), F32)],
        grid=(B, T // tq),
        in_specs=[pl.BlockSpec((1, tq, SWA_Q_DIM), lambda b, c: (b, c, 0)),
                  pl.BlockSpec((1, tq, 2 * SWA_KV_DIM), lambda b, c: (b, c, 0)),
                  pl.BlockSpec((1, WINDOW, SWA_KV_DIM), lambda b, c: (b, 0, 0)),
                  pl.BlockSpec((1, WINDOW, SWA_KV_DIM), lambda b, c: (b, 0, 0)),
                  _const_spec((1, SWA_KV_DIM)), _const_spec((1, SWA_KV_DIM)), _const_spec((1, SWA_HQ))],
        out_specs=[pl.BlockSpec((1, tq, SWA_Q_DIM), lambda b, c: (b, c, 0)),
                   pl.BlockSpec((1, WINDOW, SWA_KV_DIM), lambda b, c: (b, 0, 0)),
                   pl.BlockSpec((1, WINDOW, SWA_KV_DIM), lambda b, c: (b, 0, 0))],
        scratch_shapes=[pltpu.VMEM((WINDOW + tq, SWA_KV_DIM), F32),
                        pltpu.VMEM((WINDOW + tq, SWA_KV_DIM), F32)],
        compiler_params=_params(("parallel", "arbitrary")),
        name="swa",
    )(q, kv, k_cache, v_cache, q_gain, k_gain, sinks)


def _layer(x, mod, conv_prefix, s0, k_cache, v_cache, wts, *, bb, tt, pbb, ptt, rb, cps, tq, mask_history):
    (norm_ffn1, ffn1_in, ffn1_out, norm_mix, w_main, w_ab, conv_w, alog_pad, dtb_pad, gnorm,
     q_gain, k_gain, sinks, b_merge, w_o, norm_ffn2, ffn2_in, ffn2_out, tf) = wts
    B, T, d = x.shape
    x = _ffn(x, mod[0:3], norm_ffn1, ffn1_in, ffn1_out, bb=bb, tt=tt, tf=tf)
    conv_act, z, q_s, kv_s, gates, ab, new_conv = _inproj(x, mod[3:6], norm_mix, w_main, w_ab,
                                                          conv_prefix, conv_w, bb=pbb, tt=ptt)
    o_g, new_s = _gdn(conv_act, ab, s0, alog_pad, dtb_pad, rb=rb, cps=cps)
    o_s, new_k, new_v = _swa(q_s, kv_s, k_cache, v_cache, q_gain, k_gain, sinks,
                             tq=tq, mask_history=mask_history)
    x = _mix_ffn(x, o_g, z, gnorm, o_s, gates, b_merge, w_o, mod[3:6], mod[6:9], norm_ffn2, ffn2_in,
                 ffn2_out, bb=bb, tt=tt, tf=tf)
    new_k = new_k.reshape(B, WINDOW, SWA_HKV, SWA_HD)
    new_v = new_v.reshape(B, WINDOW, SWA_HKV, SWA_HD)
    return x, new_conv, new_s, new_k, new_v


def _pack_gate_up(w_in, tf):
    d, two_f = w_in.shape
    f = two_f // 2
    g = w_in[:, :f].reshape(d, f // tf, tf)
    u = w_in[:, f:].reshape(d, f // tf, tf)
    return jnp.concatenate([g, u], axis=-1).reshape(d, two_f).astype(BF16)


def _group_major(w, axis):
    shape = w.shape
    split = shape[:axis] + (SWA_HKV, SWA_GROUP, SWA_HD) + shape[axis + 1:]
    return jnp.swapaxes(w.reshape(split), axis, axis + 1).reshape(shape)


def _pad_lanes(v):
    return jnp.pad(v.astype(F32), (0, LANES - v.shape[0])).reshape(1, LANES)


def kernel(x_prompt, x_sample, state_gdn_conv, state_gdn, cache_swa_k, cache_swa_v, c_prompt, c_sample, w_ada, b_ada, norm_ffn1, ffn1_w_in, ffn1_w_out, norm_mix, w_in, gdn_conv_w, gdn_a_log, gdn_dt_bias, gdn_norm, swa_q_norm, swa_k_norm, swa_sinks, b_merge, w_out, norm_ffn2, ffn2_w_in, ffn2_w_out):
    depth = w_ada.shape[0]
    bp, tp, d = x_prompt.shape
    bs, ts, _ = x_sample.shape
    tf = 256
    yp, ys = x_prompt, x_sample
    outs_p, outs_s = [], []
    for l in range(depth):
        mod = _adaln(jnp.concatenate([c_prompt, c_sample], axis=0), w_ada[l], b_ada[l])
        mod = mod.reshape(N_MOD, bp + bs, 1, d)
        wl = w_in[l]
        o_conv, o_z = GDN_CONV_DIM, GDN_CONV_DIM + GDN_V_DIM
        o_a, o_b = o_z, o_z + GDN_HEADS
        o_q = o_b + GDN_HEADS
        o_k = o_q + SWA_Q_DIM
        o_g = o_k + 2 * SWA_KV_DIM
        o_gb = o_g + d
        w_main = jnp.concatenate(
            [wl[:, :o_z], _group_major(wl[:, o_q:o_k], 1), wl[:, o_k:o_gb], _group_major(wl[:, o_gb:], 1)],
            axis=1).astype(BF16)
        pad = jnp.zeros((d, LANES - GDN_HEADS), wl.dtype)
        w_ab = jnp.concatenate([wl[:, o_a:o_b], pad, wl[:, o_b:o_q], pad], axis=1).astype(BF16)
        bm = jnp.concatenate([b_merge[l][:d], _group_major(b_merge[l][d:], 0)]).reshape(1, 2 * d)
        w_o2 = jnp.concatenate([w_out[l], _group_major(w_out[l], 0)], axis=0).astype(BF16)
        wts = (norm_ffn1[l].reshape(1, d), _pack_gate_up(ffn1_w_in[l], tf), ffn1_w_out[l].astype(BF16),
               norm_mix[l].reshape(1, d), w_main, w_ab, gdn_conv_w[l],
               _pad_lanes(gdn_a_log[l]), _pad_lanes(gdn_dt_bias[l]), gdn_norm[l].reshape(1, GDN_DV),
               jnp.tile(swa_q_norm[l], SWA_HKV).reshape(1, SWA_KV_DIM),
               jnp.tile(swa_k_norm[l], SWA_HKV).reshape(1, SWA_KV_DIM),
               swa_sinks[l].reshape(1, SWA_HQ).astype(F32), bm,
               w_o2, norm_ffn2[l].reshape(1, d),
               _pack_gate_up(ffn2_w_in[l], tf), ffn2_w_out[l].astype(BF16), tf)
        zero_conv = jnp.zeros((bp, CONV_W - 1, GDN_CONV_DIM), F32)
        zero_s = jnp.zeros((bp, GDN_HEADS, GDN_DK, GDN_DV), F32)
        zero_kv = jnp.zeros((bp, WINDOW, SWA_KV_DIM), F32)
        yp, *rest_p = _layer(yp, mod[:, :bp], zero_conv, zero_s, zero_kv, zero_kv, wts,
                             bb=1, tt=512, pbb=1, ptt=256, rb=2, cps=4, tq=256, mask_history=True)
        ys, *rest_s = _layer(ys, mod[:, bp:], state_gdn_conv[l], state_gdn[l],
                             cache_swa_k[l].reshape(bs, WINDOW, SWA_KV_DIM),
                             cache_swa_v[l].reshape(bs, WINDOW, SWA_KV_DIM), wts,
                             bb=8, tt=ts, pbb=4, ptt=ts, rb=4, cps=1, tq=ts, mask_history=False)
        outs_p.append(rest_p)
        outs_s.append(rest_s)
    stack = lambda outs, i: jnp.stack([o[i] for o in outs])
    return (yp, ys,
            stack(outs_p, 0), stack(outs_p, 1), stack(outs_p, 2), stack(outs_p, 3),
            stack(outs_s, 0), stack(outs_s, 1), stack(outs_s, 2), stack(outs_s, 3))
```

```python
import functools

import jax
import jax.numpy as jnp
from jax import lax
from jax.experimental import pallas as pl
from jax.experimental.pallas import tpu as pltpu

F32 = jnp.float32
BF16 = jnp.bfloat16

CHUNK = 64
GDN_HEADS = 8
GDN_DK = 128
GDN_DV = 128
GDN_QK_DIM = GDN_HEADS * GDN_DK
GDN_V_DIM = GDN_HEADS * GDN_DV
GDN_CONV_DIM = 2 * GDN_QK_DIM + GDN_V_DIM
CONV_W = 4
SWA_HQ = 16
SWA_HKV = 4
SWA_HD = 64
SWA_GROUP = SWA_HQ // SWA_HKV
SWA_Q_DIM = SWA_HQ * SWA_HD
SWA_KV_DIM = SWA_HKV * SWA_HD
WINDOW = 128
N_MOD = 9
EPS = 1e-6
LANES = 128
VMEM_LIMIT_BYTES = 60 * 1024 * 1024


def _dot(a, b):
    return jnp.dot(a, b, preferred_element_type=F32)


def _dot_nt(a, b):
    return lax.dot_general(a, b, (((1,), (1,)), ((), ())), preferred_element_type=F32)


def _dot_tn(a, b, precision=None):
    return lax.dot_general(a, b, (((0,), (0,)), ((), ())), preferred_element_type=F32,
                           precision=precision)


def _split_bf16(x):
    hi = x.astype(BF16)
    return hi, (x - hi.astype(F32)).astype(BF16)


def _silu(x):
    return x * jax.nn.sigmoid(x)


def _softplus(x):
    return jnp.maximum(x, 0.0) + jnp.log1p(jnp.exp(-jnp.abs(x)))


def _modulate(x, gain, shift, scale):
    y = x * lax.rsqrt(jnp.mean(x * x, axis=-1, keepdims=True) + EPS)
    return (y * gain) * (1 + scale) + shift


def _const_spec(shape):
    nd = len(shape)
    return pl.BlockSpec(shape, lambda *_: (0,) * nd, pipeline_mode=pl.Buffered(1))


def _params(semantics):
    return pltpu.CompilerParams(dimension_semantics=semantics, vmem_limit_bytes=VMEM_LIMIT_BYTES)


def _adaln_kernel(c_ref, w_ref, b_ref, o_ref):
    s = _silu(c_ref[...]).astype(BF16)
    o_ref[0] = _dot(s, w_ref[...].astype(BF16)) + b_ref[0]


def _adaln(c, w_ada, b_ada):
    nb, d = c.shape
    return pl.pallas_call(
        _adaln_kernel,
        out_shape=jax.ShapeDtypeStruct((N_MOD, nb, d), F32),
        grid=(N_MOD,),
        in_specs=[pl.BlockSpec((nb, d), lambda j: (0, 0)),
                  pl.BlockSpec((d, d), lambda j: (0, j)),
                  pl.BlockSpec((1, 1, d), lambda j: (j, 0, 0))],
        out_specs=pl.BlockSpec((1, nb, d), lambda j: (j, 0, 0)),
        compiler_params=_params(("parallel",)),
        name="adaln",
    )(c, w_ada, b_ada.reshape(N_MOD, 1, d))


def _ffn_body(x, mod_ref, gain_ref, win_ref, wout_ref, act_ref, tf):
    bb, tt, d = x.shape
    d_ff = wout_ref.shape[0]
    sh, sc, gt = mod_ref[0], mod_ref[1], mod_ref[2]
    h = _modulate(x, gain_ref[...], sh, sc).reshape(bb * tt, d).astype(BF16)
    for j in range(d_ff // tf):
        gate = _dot(h, win_ref[:, j * tf:(j + 1) * tf])
        up = _dot(h, win_ref[:, d_ff + j * tf:d_ff + (j + 1) * tf])
        act_ref[:, j * tf:(j + 1) * tf] = (_silu(gate) * up).astype(BF16)
    y = _dot(act_ref[...], wout_ref[...])
    return x + (0.5 * gt) * y.reshape(bb, tt, d)


def _ffn_kernel(x_ref, mod_ref, gain_ref, win_ref, wout_ref, o_ref, act_ref, *, tf):
    o_ref[...] = _ffn_body(x_ref[...], mod_ref, gain_ref, win_ref, wout_ref, act_ref, tf)


def _mix_ffn_kernel(x_ref, og_ref, z_ref, gnorm_ref, os_ref, gates_ref, bm_ref, wo_ref, gt2_ref,
                    mod_ref, gain_ref, win_ref, wout_ref, o_ref, act_ref, *, tf):
    x = x_ref[...]
    bb, tt, d = x.shape
    gnorm = gnorm_ref[...]
    og_heads = []
    for h in range(GDN_HEADS):
        o = og_ref[:, :, h * GDN_DV:(h + 1) * GDN_DV]
        on = o * lax.rsqrt(jnp.mean(o * o, axis=-1, keepdims=True) + EPS) * gnorm
        og_heads.append(on * _silu(z_ref[:, :, h * GDN_DV:(h + 1) * GDN_DV]))
    gl = jax.nn.sigmoid(gates_ref[...] + bm_ref[...])
    mix = gl * jnp.concatenate(og_heads + [os_ref[...]], axis=-1)
    y = _dot(mix.reshape(bb * tt, 2 * d).astype(BF16), wo_ref[...]).reshape(bb, tt, d)
    x1 = x + gt2_ref[0] * y
    o_ref[...] = _ffn_body(x1, mod_ref, gain_ref, win_ref, wout_ref, act_ref, tf)


def _tok_spec(bb, tt, width):
    return pl.BlockSpec((bb, tt, width), lambda b, t: (b, t, 0))


def _mod_spec(n, bb, d):
    return pl.BlockSpec((n, bb, 1, d), lambda b, t: (0, b, 0, 0))


def _ffn(x, mod3, gain, w_in_bf16, w_out, *, bb, tt, tf):
    B, T, d = x.shape
    d_ff = w_out.shape[0]
    return pl.pallas_call(
        functools.partial(_ffn_kernel, tf=tf),
        out_shape=jax.ShapeDtypeStruct(x.shape, F32),
        grid=(B // bb, T // tt),
        in_specs=[_tok_spec(bb, tt, d), _mod_spec(3, bb, d), _const_spec((1, d)),
                  _const_spec((d, 2 * d_ff)), _const_spec((d_ff, d))],
        out_specs=_tok_spec(bb, tt, d),
        scratch_shapes=[pltpu.VMEM((bb * tt, d_ff), BF16)],
        compiler_params=_params(("parallel", "parallel")),
        name="ffn",
    )(x, mod3, gain, w_in_bf16, w_out)


def _mix_ffn(x, og, z, gnorm, os_, gates, b_merge, w_o, mod_mix, mod3, gain, w_in_bf16, w_out,
             *, bb, tt, tf):
    B, T, d = x.shape
    d_ff = w_out.shape[0]
    return pl.pallas_call(
        functools.partial(_mix_ffn_kernel, tf=tf),
        out_shape=jax.ShapeDtypeStruct(x.shape, F32),
        grid=(B // bb, T // tt),
        in_specs=[_tok_spec(bb, tt, d), _tok_spec(bb, tt, d), _tok_spec(bb, tt, d),
                  _const_spec((1, GDN_DV)), _tok_spec(bb, tt, d),
                  _tok_spec(bb, tt, 2 * d), _const_spec((1, 2 * d)), _const_spec((2 * d, d)),
                  pl.BlockSpec((1, bb, 1, d), lambda b, t: (2, b, 0, 0)),
                  _mod_spec(3, bb, d), _const_spec((1, d)),
                  _const_spec((d, 2 * d_ff)), _const_spec((d_ff, d))],
        out_specs=_tok_spec(bb, tt, d),
        scratch_shapes=[pltpu.VMEM((bb * tt, d_ff), BF16)],
        compiler_params=_params(("parallel", "parallel")),
        name="mix_ffn",
    )(x, og, z, gnorm, os_, gates, b_merge, w_o, mod_mix, mod3, gain, w_in_bf16, w_out)


_PROJ_STEP = 512
_AB_WIDTH = 2 * LANES
_XP_PAD = 8


def _inproj_kernel(x_ref, mod_ref, gain_ref, w_ref, wab_ref, prefix_ref, convw_ref,
                   conv_ref, z_ref, q_ref, kv_ref, gates_ref, ab_ref, convout_ref, xp_scr):
    t_idx = pl.program_id(1)
    x = x_ref[...]
    bb, tt, d = x.shape
    hist = CONV_W - 1

    @pl.when(t_idx == 0)
    def _():
        xp_scr[:, _XP_PAD - hist:_XP_PAD, :] = prefix_ref[...]

    h = _modulate(x, gain_ref[...], mod_ref[0], mod_ref[1]).reshape(bb * tt, d).astype(BF16)
    cw = convw_ref[...]
    for lo in range(0, GDN_CONV_DIM, _PROJ_STEP):
        hi = lo + _PROJ_STEP
        pre = _dot(h, w_ref[:, lo:hi]).reshape(bb, tt, hi - lo)
        xp_scr[:, _XP_PAD:_XP_PAD + tt, lo:hi] = pre
        y = pre * cw[hist:hist + 1, lo:hi]
        for i in range(hist):
            y = y + xp_scr[:, _XP_PAD - hist + i:_XP_PAD - hist + i + tt, lo:hi] * cw[i:i + 1, lo:hi]
        conv_ref[:, :, lo:hi] = _silu(y)
    tail = xp_scr[:, tt:tt + _XP_PAD, :]
    xp_scr[:, 0:_XP_PAD, :] = tail

    col = GDN_CONV_DIM
    for ref in (z_ref, q_ref, kv_ref, gates_ref):
        width = ref.shape[-1]
        for lo in range(0, width, _PROJ_STEP):
            hi = min(lo + _PROJ_STEP, width)
            ref[:, :, lo:hi] = _dot(h, w_ref[:, col + lo:col + hi]).reshape(bb, tt, hi - lo)
        col += width
    ab_ref[...] = _dot(h, wab_ref[...]).reshape(bb, tt, _AB_WIDTH)

    @pl.when(t_idx == pl.num_programs(1) - 1)
    def _():
        convout_ref[...] = tail[:, _XP_PAD - hist:, :]


def _inproj(x, mod3, gain, w_main, w_ab, prefix, conv_w, *, bb, tt):
    B, T, d = x.shape
    widths = (GDN_CONV_DIM, GDN_V_DIM, SWA_Q_DIM, 2 * SWA_KV_DIM, 2 * d, _AB_WIDTH)
    hist_spec = pl.BlockSpec((bb, CONV_W - 1, GDN_CONV_DIM), lambda b, t: (b, 0, 0))
    return pl.pallas_call(
        _inproj_kernel,
        out_shape=[jax.ShapeDtypeStruct((B, T, w), F32) for w in widths]
        + [jax.ShapeDtypeStruct((B, CONV_W - 1, GDN_CONV_DIM), F32)],
        grid=(B // bb, T // tt),
        in_specs=[_tok_spec(bb, tt, d), _mod_spec(3, bb, d), _const_spec((1, d)),
                  _const_spec(w_main.shape), _const_spec(w_ab.shape), hist_spec,
                  _const_spec(conv_w.shape)],
        out_specs=[_tok_spec(bb, tt, w) for w in widths] + [hist_spec],
        scratch_shapes=[pltpu.VMEM((bb, _XP_PAD + tt, GDN_CONV_DIM), F32)],
        compiler_params=_params(("parallel", "arbitrary")),
        name="inproj",
    )(x, mod3, gain, w_main, w_ab, prefix, conv_w)


GDN_PACK = 4


def _pack_masks(size, pk):
    row = lax.broadcasted_iota(jnp.int32, (size, pk * size), 0)
    lane = lax.broadcasted_iota(jnp.int32, (size, pk * size), 1)
    col = lane % size
    merges = []
    s = 1
    while s < size:
        rb, cb = row // s, col // s
        merges.append((rb == cb + 1) & (rb % 2 == 1))
        s *= 2
    return dict(seg=lane // size, eye=row == col, incl=row >= col, strict=row > col, merges=merges)


def _select_seg(seg, parts):
    out = parts[-1]
    for k in range(len(parts) - 2, -1, -1):
        out = jnp.where(seg == k, parts[k], out)
    return out


def _block_diag_packed(m, seg, pk):
    zero = jnp.zeros_like(m)
    return jnp.concatenate([jnp.where(seg == k, m, zero) for k in range(pk)], axis=0)


def _block_diag(blocks):
    n = len(blocks)
    zero = jnp.zeros_like(blocks[0])
    return jnp.concatenate(
        [jnp.concatenate([blocks[k] if j == k else zero for j in range(n)], axis=1) for k in range(n)],
        axis=0)


def _gdn_kernel(conv_ref, ab_ref, s0_ref, alog_ref, dtb_ref, o_ref, sout_ref, s_scr, *, rb, cps):
    c = pl.program_id(1)
    L = CHUNK
    pk = GDN_PACK
    ngrp = GDN_HEADS // pk

    @pl.when(c == 0)
    def _():
        s_scr[...] = s0_ref[...]

    mk = _pack_masks(L, pk)
    seg, incl, strict, merges = mk["seg"], mk["incl"], mk["strict"], mk["merges"]
    row64 = lax.broadcasted_iota(jnp.int32, (L, L), 0)
    col64 = lax.broadcasted_iota(jnp.int32, (L, L), 1)
    incl64 = jnp.where(row64 >= col64, 1.0, 0.0).astype(BF16)
    incl3 = jnp.concatenate([incl64] * 3, axis=1)
    upper_p = jnp.where(strict, 0.0, 1.0).astype(BF16)
    upper3 = jnp.concatenate([upper_p] * 3, axis=0)

    chunks = [(r, i) for r in range(rb) for i in range(cps)]
    units = [(r, i, p) for (r, i) in chunks for p in range(ngrp)]
    heads = [(r, i, h) for (r, i) in chunks for h in range(GDN_HEADS)]

    def head_cols(off, r, i, h):
        return conv_ref[r, i * L:(i + 1) * L, off + h * GDN_DK:off + (h + 1) * GDN_DK]

    beta, gc, g2 = {}, {}, {}
    for (r, i) in chunks:
        ab = ab_ref[r, i * L:(i + 1) * L, :]
        g = -jnp.exp(alog_ref[...]) * _softplus(ab[:, :LANES] + dtb_ref[...])
        beta[r, i] = jax.nn.sigmoid(ab[:, LANES:])
        g_hi = g.astype(BF16)
        g_mid, g_lo = _split_bf16(g - g_hi.astype(F32))
        g3 = jnp.concatenate([g_hi, g_mid, g_lo], axis=0)
        gc[r, i] = _dot(incl3, g3)
        g2[r, i] = _dot_tn(g3, upper3)

    qn, kn, vb, qd, kdec, dl, gcb, btb = {}, {}, {}, {}, {}, {}, {}, {}
    for (r, i, h) in heads:
        qh, kh = head_cols(0, r, i, h), head_cols(GDN_QK_DIM, r, i, h)
        qn[r, i, h] = qh * lax.rsqrt(jnp.sum(qh * qh, axis=-1, keepdims=True) + EPS) * (GDN_DK ** -0.5)
        kn[r, i, h] = kh * lax.rsqrt(jnp.sum(kh * kh, axis=-1, keepdims=True) + EPS)
        gcb[r, i, h] = jnp.broadcast_to(gc[r, i][:, h:h + 1], (L, LANES))
        btb[r, i, h] = jnp.broadcast_to(beta[r, i][:, h:h + 1], (L, LANES))
        egb = jnp.exp(gcb[r, i, h])
        glb = gcb[r, i, h][L - 1:L, :]
        vb[r, i, h] = jnp.concatenate([head_cols(2 * GDN_QK_DIM, r, i, h) * btb[r, i, h],
                                       kn[r, i, h] * (btb[r, i, h] * egb)], axis=-1).astype(BF16)
        qd[r, i, h] = qn[r, i, h] * egb
        kdec[r, i, h] = (kn[r, i, h] * jnp.exp(glb - gcb[r, i, h])).astype(BF16)
        dl[r, i, h] = jnp.exp(glb)

    def packed(cols):
        reps = (pk * L) // LANES
        return _select_seg(seg, [jnp.concatenate([cc] * reps, axis=1) for cc in cols])

    gamma, n_p, qkg = {}, {}, {}
    for (r, i, p) in units:
        hs = range(pk * p, pk * (p + 1))
        gr_p = _select_seg(seg[0:1], [g2[r, i][h:h + 1, :] for h in hs])
        gamma[r, i, p] = jnp.exp(jnp.where(incl, packed([gcb[r, i, h] for h in hs]) - gr_p, -jnp.inf))
    for (r, i, p) in units:
        hs = range(pk * p, pk * (p + 1))
        kb = [kn[r, i, h].astype(BF16) for h in hs]
        lhs = jnp.concatenate([jnp.concatenate(kb, axis=1),
                               jnp.concatenate([qn[r, i, h].astype(BF16) for h in hs], axis=1)],
                              axis=0)
        res = _dot_nt(lhs, _block_diag(kb))
        bt_p = packed([btb[r, i, h] for h in hs])
        n_p[r, i, p] = jnp.where(strict, bt_p * res[:L] * gamma[r, i, p], 0.0)
        qkg[r, i, p] = (res[L:] * gamma[r, i, p]).astype(BF16)

    t = {u: jnp.where(mk["eye"], 1.0, 0.0) - jnp.where(merges[0], n_p[u], 0.0) for u in units}
    for m in merges[1:]:
        ct = {u: _dot(jnp.where(m, n_p[u], 0.0).astype(BF16), _block_diag_packed(t[u].astype(BF16), seg, pk))
              for u in units}
        t = {u: t[u] - _dot(t[u].astype(BF16), _block_diag_packed(ct[u].astype(BF16), seg, pk))
             for u in units}

    u_h, w_h = {}, {}
    for (r, i, p) in units:
        hs = range(pk * p, pk * (p + 1))
        uw = _dot(t[r, i, p].astype(BF16), _block_diag([vb[r, i, h] for h in hs]))
        for k, h in enumerate(hs):
            u_h[r, i, h] = uw[:, 2 * k * GDN_DV:(2 * k + 1) * GDN_DV]
            w_h[r, i, h] = uw[:, (2 * k + 1) * GDN_DV:(2 * k + 2) * GDN_DV]

    pairs = [(a, a + 1) for a in range(0, GDN_HEADS, 2)]
    wq = {}
    for (r, i) in chunks:
        for (a, b) in pairs:
            wq[r, i, a] = jnp.concatenate(
                [jnp.concatenate([w_h[r, i, a], w_h[r, i, b]], axis=1),
                 jnp.concatenate([qd[r, i, a], qd[r, i, b]], axis=1)],
                axis=0).astype(BF16)
    s_cur = {(r, h): s_scr[r, h] for r in range(rb) for h in range(GDN_HEADS)}
    for i in range(cps):
        rs, vn = {}, {}
        for r in range(rb):
            for (a, b) in pairs:
                rs[r, a] = _dot(wq[r, i, a], _block_diag([s_cur[r, a].astype(BF16), s_cur[r, b].astype(BF16)]))
        for r in range(rb):
            for (a, b) in pairs:
                vn[r, a] = (u_h[r, i, a] - rs[r, a][:L, :GDN_DV]).astype(BF16)
                vn[r, b] = (u_h[r, i, b] - rs[r, a][:L, GDN_DV:]).astype(BF16)
        for r in range(rb):
            for (a, b) in pairs:
                lo = (a % pk) * L
                o2 = _dot(qkg[r, i, a // pk][:, lo:lo + 2 * L], _block_diag([vn[r, a], vn[r, b]]))
                o_ref[r, i * L:(i + 1) * L, a * GDN_DV:(a + 2) * GDN_DV] = rs[r, a][L:, :] + o2
        for r in range(rb):
            for h in range(GDN_HEADS):
                s_cur[r, h] = dl[r, i, h] * s_cur[r, h] + _dot_tn(kdec[r, i, h], vn[r, h])

    for r in range(rb):
        for h in range(GDN_HEADS):
            s_scr[r, h] = s_cur[r, h]

    @pl.when(c == pl.num_programs(1) - 1)
    def _():
        sout_ref[...] = s_scr[...]


def _gdn(conv_act, ab, s0, alog_pad, dtb_pad, *, rb, cps):
    B, T, cd = conv_act.shape
    tg = cps * CHUNK
    state_spec = pl.BlockSpec((rb,) + s0.shape[1:], lambda b, c: (b, 0, 0, 0))
    return pl.pallas_call(
        functools.partial(_gdn_kernel, rb=rb, cps=cps),
        out_shape=[jax.ShapeDtypeStruct((B, T, GDN_V_DIM), F32),
                   jax.ShapeDtypeStruct(s0.shape, F32)],
        grid=(B // rb, T // tg),
        in_specs=[pl.BlockSpec((rb, tg, cd), lambda b, c: (b, c, 0)),
                  pl.BlockSpec((rb, tg, _AB_WIDTH), lambda b, c: (b, c, 0)),
                  state_spec, _const_spec((1, LANES)), _const_spec((1, LANES))],
        out_specs=[pl.BlockSpec((rb, tg, GDN_V_DIM), lambda b, c: (b, c, 0)), state_spec],
        scratch_shapes=[pltpu.VMEM((rb,) + s0.shape[1:], F32)],
        compiler_params=_params(("parallel", "arbitrary")),
        name="gdn",
    )(conv_act, ab, s0, alog_pad, dtb_pad)


def _segment_rms(x, gain, ones2):
    hi, lo = _split_bf16(x * x)
    ss = _dot(jnp.concatenate([hi, lo], axis=1), ones2)
    return x * lax.rsqrt(ss * (1.0 / SWA_HD) + EPS) * gain


def _swa_kernel(q_ref, kv_ref, kc_ref, vc_ref, qg_ref, kg_ref, sink_ref,
                o_ref, kout_ref, vout_ref, kwin, vwin, *, rb, tq, mask_history):
    c = pl.program_id(1)
    L = CHUNK
    span = WINDOW + L
    kvd = SWA_KV_DIM
    nj = tq // L

    @pl.when(c == 0)
    def _():
        kwin[:, 0:WINDOW, :] = kc_ref[...]
        vwin[:, 0:WINDOW, :] = vc_ref[...]

    seg_r = lax.broadcasted_iota(jnp.int32, (kvd, kvd), 0) // SWA_HD
    seg_c = lax.broadcasted_iota(jnp.int32, (kvd, kvd), 1) // SWA_HD
    ones_bd = jnp.where(seg_r == seg_c, 1.0, 0.0).astype(BF16)
    ones2 = jnp.concatenate([ones_bd, ones_bd], axis=0)

    rows = SWA_GROUP * L
    scale = SWA_HD ** -0.5
    qgain = qg_ref[...] * scale
    sink_all = jnp.concatenate([jnp.broadcast_to(sink_ref[:, h:h + 1], (L, 1)) for h in range(SWA_HQ)],
                               axis=0)
    seg_o = lax.broadcasted_iota(jnp.int32, (rows, kvd), 1) // SWA_HD
    seg_w = lax.broadcasted_iota(jnp.int32, (WINDOW + tq, kvd), 1) // SWA_HD

    for r in range(rb):
        kv = kv_ref[r]
        kwin[r, WINDOW:WINDOW + tq, :] = _segment_rms(kv[:, :kvd], kg_ref[...], ones2)
        vwin[r, WINDOW:WINDOW + tq, :] = kv[:, kvd:]

    k_only, v_only, qn = {}, {}, {}
    for r in range(rb):
        kw, vw = kwin[r], vwin[r]
        for hk in range(SWA_HKV):
            k_only[r, hk] = jnp.where(seg_w == hk, kw, 0.0).astype(BF16)
            v_only[r, hk] = jnp.where(seg_w == hk, vw, 0.0).astype(BF16)
        for j in range(nj):
            q_stack = jnp.concatenate([q_ref[r, j * L:(j + 1) * L, g * kvd:(g + 1) * kvd]
                                       for g in range(SWA_GROUP)], axis=0)
            qn[r, j] = _segment_rms(q_stack, qgain, ones2).astype(BF16)

    def scores(r, j):
        return [_dot_nt(qn[r, j], k_only[r, hk][j * L:j * L + span]) for hk in range(SWA_HKV)]

    steps = [(r, j) for r in range(rb) for j in range(nj)]
    s_next = scores(*steps[0])
    for n, (r, j) in enumerate(steps):
        s_all = jnp.concatenate(s_next, axis=0)
        if n + 1 < len(steps):
            s_next = scores(*steps[n + 1])
        if mask_history:
            kpos = c * tq + (j * L - WINDOW) + lax.broadcasted_iota(jnp.int32, s_all.shape, 1)
            s_all = jnp.where(kpos >= 0, s_all, -jnp.inf)
        m = jnp.maximum(jnp.max(s_all, axis=-1, keepdims=True), sink_all)
        p = jnp.exp(s_all - m)
        den = jnp.sum(p, axis=-1, keepdims=True) + jnp.exp(sink_all - m)
        pb = p.astype(BF16)
        acc = None
        den_full = None
        for hk in range(SWA_HKV):
            pv = _dot(pb[hk * rows:(hk + 1) * rows], v_only[r, hk][j * L:j * L + span])
            acc = pv if acc is None else acc + pv
            dh = den[hk * rows:(hk + 1) * rows]
            den_full = dh if den_full is None else jnp.where(seg_o == hk, dh, den_full)
        o = acc / den_full
        for g in range(SWA_GROUP):
            o_ref[r, j * L:(j + 1) * L, g * kvd:(g + 1) * kvd] = o[g * L:(g + 1) * L]

    knew = kwin[:, tq:tq + WINDOW, :]
    vnew = vwin[:, tq:tq + WINDOW, :]
    kwin[:, 0:WINDOW, :] = knew
    vwin[:, 0:WINDOW, :] = vnew

    @pl.when(c == pl.num_programs(1) - 1)
    def _():
        kout_ref[...] = knew
        vout_ref[...] = vnew


def _swa(q, kv, k_cache, v_cache, q_gain, k_gain, sinks, *, rb, tq, mask_history):
    B, T, _ = q.shape
    tok = lambda w: pl.BlockSpec((rb, tq, w), lambda b, c: (b, c, 0))
    win = pl.BlockSpec((rb, WINDOW, SWA_KV_DIM), lambda b, c: (b, 0, 0))
    return pl.pallas_call(
        functools.partial(_swa_kernel, rb=rb, tq=tq, mask_history=mask_history),
        out_shape=[jax.ShapeDtypeStruct((B, T, SWA_Q_DIM), F32),
                   jax.ShapeDtypeStruct((B, WINDOW, SWA_KV_DIM), F32),
                   jax.ShapeDtypeStruct((B, WINDOW, SWA_KV_DIM), F32)],
        grid=(B // rb, T // tq),
        in_specs=[tok(SWA_Q_DIM), tok(2 * SWA_KV_DIM), win, win,
                  _const_spec((1, SWA_KV_DIM)), _const_spec((1, SWA_KV_DIM)), _const_spec((1, SWA_HQ))],
        out_specs=[tok(SWA_Q_DIM), win, win],
        scratch_shapes=[pltpu.VMEM((rb, WINDOW + tq, SWA_KV_DIM), F32),
                        pltpu.VMEM((rb, WINDOW + tq, SWA_KV_DIM), F32)],
        compiler_params=_params(("parallel", "arbitrary")),
        name="swa",
    )(q, kv, k_cache, v_cache, q_gain, k_gain, sinks)


_FFN_ROWS = 512
_PROJ_ROWS = 256
_GDN_CHUNK_ROWS = 8
_SWA_QUERIES = 512


def _tile_plan(batch, seq):
    def rows_by_time(target):
        tt = min(seq, target)
        return max(1, min(batch, target // tt)), tt
    bb, tt = rows_by_time(_FFN_ROWS)
    pbb, ptt = rows_by_time(_PROJ_ROWS)
    cps = min(seq // CHUNK, 4)
    rb = max(1, min(batch, _GDN_CHUNK_ROWS // cps))
    srb, tq = rows_by_time(_SWA_QUERIES)
    srb = min(srb, 4)
    return dict(bb=bb, tt=tt, pbb=pbb, ptt=ptt, rb=rb, cps=cps, srb=srb, tq=tq)


def _layer(x, mod, conv_prefix, s0, k_cache, v_cache, wts, *, mask_history):
    bb, tt, pbb, ptt, rb, cps, srb, tq = (_tile_plan(x.shape[0], x.shape[1])[k] for k in
                                            ("bb", "tt", "pbb", "ptt", "rb", "cps", "srb", "tq"))
    (norm_ffn1, ffn1_in, ffn1_out, norm_mix, w_main, w_ab, conv_w, alog_pad, dtb_pad, gnorm,
     q_gain, k_gain, sinks, b_merge, w_o, norm_ffn2, ffn2_in, ffn2_out, tf) = wts
    B, T, d = x.shape
    x = _ffn(x, mod[0:3], norm_ffn1, ffn1_in, ffn1_out, bb=bb, tt=tt, tf=tf)
    conv_act, z, q_s, kv_s, gates, ab, new_conv = _inproj(x, mod[3:6], norm_mix, w_main, w_ab,
                                                          conv_prefix, conv_w, bb=pbb, tt=ptt)
    o_g, new_s = _gdn(conv_act, ab, s0, alog_pad, dtb_pad, rb=rb, cps=cps)
    o_s, new_k, new_v = _swa(q_s, kv_s, k_cache, v_cache, q_gain, k_gain, sinks,
                             rb=srb, tq=tq, mask_history=mask_history)
    x = _mix_ffn(x, o_g, z, gnorm, o_s, gates, b_merge, w_o, mod[3:6], mod[6:9], norm_ffn2, ffn2_in,
                 ffn2_out, bb=bb, tt=tt, tf=tf)
    new_k = new_k.reshape(B, WINDOW, SWA_HKV, SWA_HD)
    new_v = new_v.reshape(B, WINDOW, SWA_HKV, SWA_HD)
    return x, new_conv, new_s, new_k, new_v


def _group_major(w, axis):
    shape = w.shape
    split = shape[:axis] + (SWA_HKV, SWA_GROUP, SWA_HD) + shape[axis + 1:]
    return jnp.swapaxes(w.reshape(split), axis, axis + 1).reshape(shape)


def _pad_lanes(v):
    return jnp.pad(v.astype(F32), (0, LANES - v.shape[0])).reshape(1, LANES)


def kernel(x_prompt, x_sample, state_gdn_conv, state_gdn, cache_swa_k, cache_swa_v, c_prompt, c_sample, w_ada, b_ada, norm_ffn1, ffn1_w_in, ffn1_w_out, norm_mix, w_in, gdn_conv_w, gdn_a_log, gdn_dt_bias, gdn_norm, swa_q_norm, swa_k_norm, swa_sinks, b_merge, w_out, norm_ffn2, ffn2_w_in, ffn2_w_out):
    depth = w_ada.shape[0]
    bp, tp, d = x_prompt.shape
    bs, ts, _ = x_sample.shape
    tf = 256
    yp, ys = x_prompt, x_sample
    outs_p, outs_s = [], []
    for l in range(depth):
        mod = _adaln(jnp.concatenate([c_prompt, c_sample], axis=0), w_ada[l], b_ada[l])
        mod = mod.reshape(N_MOD, bp + bs, 1, d)
        wl = w_in[l]
        o_conv, o_z = GDN_CONV_DIM, GDN_CONV_DIM + GDN_V_DIM
        o_a, o_b = o_z, o_z + GDN_HEADS
        o_q = o_b + GDN_HEADS
        o_k = o_q + SWA_Q_DIM
        o_g = o_k + 2 * SWA_KV_DIM
        o_gb = o_g + d
        w_main = jnp.concatenate(
            [wl[:, :o_z], _group_major(wl[:, o_q:o_k], 1), wl[:, o_k:o_gb], _group_major(wl[:, o_gb:], 1)],
            axis=1).astype(BF16)
        pad = jnp.zeros((d, LANES - GDN_HEADS), wl.dtype)
        w_ab = jnp.concatenate([wl[:, o_a:o_b], pad, wl[:, o_b:o_q], pad], axis=1).astype(BF16)
        bm = jnp.concatenate([b_merge[l][:d], _group_major(b_merge[l][d:], 0)]).reshape(1, 2 * d)
        w_o2 = jnp.concatenate([w_out[l], _group_major(w_out[l], 0)], axis=0).astype(BF16)
        wts = (norm_ffn1[l].reshape(1, d), ffn1_w_in[l].astype(BF16), ffn1_w_out[l].astype(BF16),
               norm_mix[l].reshape(1, d), w_main, w_ab, gdn_conv_w[l],
               _pad_lanes(gdn_a_log[l]), _pad_lanes(gdn_dt_bias[l]), gdn_norm[l].reshape(1, GDN_DV),
               jnp.tile(swa_q_norm[l], SWA_HKV).reshape(1, SWA_KV_DIM),
               jnp.tile(swa_k_norm[l], SWA_HKV).reshape(1, SWA_KV_DIM),
               swa_sinks[l].reshape(1, SWA_HQ).astype(F32), bm,
               w_o2, norm_ffn2[l].reshape(1, d),
               ffn2_w_in[l].astype(BF16), ffn2_w_out[l].astype(BF16), tf)
        zero_conv = jnp.zeros((bp, CONV_W - 1, GDN_CONV_DIM), F32)
        zero_s = jnp.zeros((bp, GDN_HEADS, GDN_DK, GDN_DV), F32)
        zero_kv = jnp.zeros((bp, WINDOW, SWA_KV_DIM), F32)
        yp, *rest_p = _layer(yp, mod[:, :bp], zero_conv, zero_s, zero_kv, zero_kv, wts, mask_history=True)
        ys, *rest_s = _layer(ys, mod[:, bp:], state_gdn_conv[l], state_gdn[l],
                             cache_swa_k[l].reshape(bs, WINDOW, SWA_KV_DIM),
                             cache_swa_v[l].reshape(bs, WINDOW, SWA_KV_DIM), wts, mask_history=False)
        outs_p.append(rest_p)
        outs_s.append(rest_s)
    stack = lambda outs, i: jnp.stack([o[i] for o in outs])
    return (yp, ys,
            stack(outs_p, 0), stack(outs_p, 1), stack(outs_p, 2), stack(outs_p, 3),
            stack(outs_s, 0), stack(outs_s, 1), stack(outs_s, 2), stack(outs_s, 3))
```

```python
import functools

import jax
import jax.numpy as jnp
from jax import lax
from jax.experimental import pallas as pl
from jax.experimental.pallas import tpu as pltpu

F32 = jnp.float32
BF16 = jnp.bfloat16

CHUNK = 64
GDN_HEADS = 8
GDN_DK = 128
GDN_DV = 128
GDN_QK_DIM = GDN_HEADS * GDN_DK
GDN_V_DIM = GDN_HEADS * GDN_DV
GDN_CONV_DIM = 2 * GDN_QK_DIM + GDN_V_DIM
CONV_W = 4
SWA_HQ = 16
SWA_HKV = 4
SWA_HD = 64
SWA_GROUP = SWA_HQ // SWA_HKV
SWA_Q_DIM = SWA_HQ * SWA_HD
SWA_KV_DIM = SWA_HKV * SWA_HD
WINDOW = 128
N_MOD = 9
EPS = 1e-6
LANES = 128
VMEM_LIMIT_BYTES = 60 * 1024 * 1024


def _dot(a, b):
    return jnp.dot(a, b, preferred_element_type=F32)


def _dot_nt(a, b):
    return lax.dot_general(a, b, (((1,), (1,)), ((), ())), preferred_element_type=F32)


def _dot_tn(a, b, precision=None):
    return lax.dot_general(a, b, (((0,), (0,)), ((), ())), preferred_element_type=F32,
                           precision=precision)


def _split_bf16(x):
    hi = x.astype(BF16)
    return hi, (x - hi.astype(F32)).astype(BF16)


def _silu(x):
    return x * jax.nn.sigmoid(x)


def _softplus(x):
    return jnp.maximum(x, 0.0) + jnp.log1p(jnp.exp(-jnp.abs(x)))


def _modulate(x, gain, shift, scale):
    y = x * lax.rsqrt(jnp.mean(x * x, axis=-1, keepdims=True) + EPS)
    return (y * gain) * (1 + scale) + shift


def _const_spec(shape):
    nd = len(shape)
    return pl.BlockSpec(shape, lambda *_: (0,) * nd, pipeline_mode=pl.Buffered(1))


def _params(semantics):
    return pltpu.CompilerParams(dimension_semantics=semantics, vmem_limit_bytes=VMEM_LIMIT_BYTES)


def _adaln_kernel(c_ref, w_ref, b_ref, o_ref):
    s = _silu(c_ref[...]).astype(BF16)
    o_ref[0] = _dot(s, w_ref[...].astype(BF16)) + b_ref[0]


def _adaln(c, w_ada, b_ada):
    nb, d = c.shape
    return pl.pallas_call(
        _adaln_kernel,
        out_shape=jax.ShapeDtypeStruct((N_MOD, nb, d), F32),
        grid=(N_MOD,),
        in_specs=[pl.BlockSpec((nb, d), lambda j: (0, 0)),
                  pl.BlockSpec((d, d), lambda j: (0, j)),
                  pl.BlockSpec((1, 1, d), lambda j: (j, 0, 0))],
        out_specs=pl.BlockSpec((1, nb, d), lambda j: (j, 0, 0)),
        compiler_params=_params(("parallel",)),
        name="adaln",
    )(c, w_ada, b_ada.reshape(N_MOD, 1, d))


def _ffn_body(x, mod_ref, gain_ref, win_ref, wout_ref, act_ref, tf):
    bb, tt, d = x.shape
    d_ff = wout_ref.shape[0]
    sh, sc, gt = mod_ref[0], mod_ref[1], mod_ref[2]
    h = _modulate(x, gain_ref[...], sh, sc).reshape(bb * tt, d).astype(BF16)
    for j in range(d_ff // tf):
        gate = _dot(h, win_ref[:, j * tf:(j + 1) * tf])
        up = _dot(h, win_ref[:, d_ff + j * tf:d_ff + (j + 1) * tf])
        act_ref[:, j * tf:(j + 1) * tf] = (_silu(gate) * up).astype(BF16)
    y = _dot(act_ref[...], wout_ref[...])
    return x + (0.5 * gt) * y.reshape(bb, tt, d)


def _ffn_kernel(x_ref, mod_ref, gain_ref, win_ref, wout_ref, o_ref, act_ref, *, tf):
    o_ref[...] = _ffn_body(x_ref[...], mod_ref, gain_ref, win_ref, wout_ref, act_ref, tf)


def _mix_ffn_kernel(x_ref, og_ref, z_ref, gnorm_ref, os_ref, gates_ref, bm_ref, wo_ref, gt2_ref,
                    mod_ref, gain_ref, win_ref, wout_ref, o_ref, act_ref, *, tf):
    x = x_ref[...]
    bb, tt, d = x.shape
    gnorm = gnorm_ref[...]
    og_heads = []
    for h in range(GDN_HEADS):
        o = og_ref[:, :, h * GDN_DV:(h + 1) * GDN_DV]
        on = o * lax.rsqrt(jnp.mean(o * o, axis=-1, keepdims=True) + EPS) * gnorm
        og_heads.append(on * _silu(z_ref[:, :, h * GDN_DV:(h + 1) * GDN_DV]))
    gl = jax.nn.sigmoid(gates_ref[...] + bm_ref[...])
    mix = gl * jnp.concatenate(og_heads + [os_ref[...]], axis=-1)
    y = _dot(mix.reshape(bb * tt, 2 * d).astype(BF16), wo_ref[...]).reshape(bb, tt, d)
    x1 = x + gt2_ref[0] * y
    o_ref[...] = _ffn_body(x1, mod_ref, gain_ref, win_ref, wout_ref, act_ref, tf)


def _tok_spec(bb, tt, width):
    return pl.BlockSpec((bb, tt, width), lambda b, t: (b, t, 0))


def _mod_spec(n, bb, d):
    return pl.BlockSpec((n, bb, 1, d), lambda b, t: (0, b, 0, 0))


def _ffn(x, mod3, gain, w_in_bf16, w_out, *, bb, tt, tf):
    B, T, d = x.shape
    d_ff = w_out.shape[0]
    return pl.pallas_call(
        functools.partial(_ffn_kernel, tf=tf),
        out_shape=jax.ShapeDtypeStruct(x.shape, F32),
        grid=(B // bb, T // tt),
        in_specs=[_tok_spec(bb, tt, d), _mod_spec(3, bb, d), _const_spec((1, d)),
                  _const_spec((d, 2 * d_ff)), _const_spec((d_ff, d))],
        out_specs=_tok_spec(bb, tt, d),
        scratch_shapes=[pltpu.VMEM((bb * tt, d_ff), BF16)],
        compiler_params=_params(("parallel", "parallel")),
        name="ffn",
    )(x, mod3, gain, w_in_bf16, w_out)


def _mix_ffn(x, og, z, gnorm, os_, gates, b_merge, w_o, mod_mix, mod3, gain, w_in_bf16, w_out,
             *, bb, tt, tf):
    B, T, d = x.shape
    d_ff = w_out.shape[0]
    return pl.pallas_call(
        functools.partial(_mix_ffn_kernel, tf=tf),
        out_shape=jax.ShapeDtypeStruct(x.shape, F32),
        grid=(B // bb, T // tt),
        in_specs=[_tok_spec(bb, tt, d), _tok_spec(bb, tt, d), _tok_spec(bb, tt, d),
                  _const_spec((1, GDN_DV)), _tok_spec(bb, tt, d),
                  _tok_spec(bb, tt, 2 * d), _const_spec((1, 2 * d)), _const_spec((2 * d, d)),
                  pl.BlockSpec((1, bb, 1, d), lambda b, t: (2, b, 0, 0)),
                  _mod_spec(3, bb, d), _const_spec((1, d)),
                  _const_spec((d, 2 * d_ff)), _const_spec((d_ff, d))],
        out_specs=_tok_spec(bb, tt, d),
        scratch_shapes=[pltpu.VMEM((bb * tt, d_ff), BF16)],
        compiler_params=_params(("parallel", "parallel")),
        name="mix_ffn",
    )(x, og, z, gnorm, os_, gates, b_merge, w_o, mod_mix, mod3, gain, w_in_bf16, w_out)


_PROJ_STEP = 512
_AB_WIDTH = 2 * LANES
_XP_PAD = 8


def _inproj_kernel(x_ref, mod_ref, gain_ref, w_ref, wab_ref, prefix_ref, convw_ref,
                   conv_ref, z_ref, q_ref, kv_ref, gates_ref, ab_ref, convout_ref, xp_scr):
    t_idx = pl.program_id(1)
    x = x_ref[...]
    bb, tt, d = x.shape
    hist = CONV_W - 1

    @pl.when(t_idx == 0)
    def _():
        xp_scr[:, _XP_PAD - hist:_XP_PAD, :] = prefix_ref[...]

    h = _modulate(x, gain_ref[...], mod_ref[0], mod_ref[1]).reshape(bb * tt, d).astype(BF16)
    cw = convw_ref[...]
    for lo in range(0, GDN_CONV_DIM, _PROJ_STEP):
        hi = lo + _PROJ_STEP
        pre = _dot(h, w_ref[:, lo:hi]).reshape(bb, tt, hi - lo)
        xp_scr[:, _XP_PAD:_XP_PAD + tt, lo:hi] = pre
        y = pre * cw[hist:hist + 1, lo:hi]
        for i in range(hist):
            y = y + xp_scr[:, _XP_PAD - hist + i:_XP_PAD - hist + i + tt, lo:hi] * cw[i:i + 1, lo:hi]
        conv_ref[:, :, lo:hi] = _silu(y)
    tail = xp_scr[:, tt:tt + _XP_PAD, :]
    xp_scr[:, 0:_XP_PAD, :] = tail

    col = GDN_CONV_DIM
    for ref in (z_ref, q_ref, kv_ref, gates_ref):
        width = ref.shape[-1]
        for lo in range(0, width, _PROJ_STEP):
            hi = min(lo + _PROJ_STEP, width)
            ref[:, :, lo:hi] = _dot(h, w_ref[:, col + lo:col + hi]).reshape(bb, tt, hi - lo)
        col += width
    ab_ref[...] = _dot(h, wab_ref[...]).reshape(bb, tt, _AB_WIDTH)

    @pl.when(t_idx == pl.num_programs(1) - 1)
    def _():
        convout_ref[...] = tail[:, _XP_PAD - hist:, :]


def _inproj(x, mod3, gain, w_main, w_ab, prefix, conv_w, *, bb, tt):
    B, T, d = x.shape
    widths = (GDN_CONV_DIM, GDN_V_DIM, SWA_Q_DIM, 2 * SWA_KV_DIM, 2 * d, _AB_WIDTH)
    hist_spec = pl.BlockSpec((bb, CONV_W - 1, GDN_CONV_DIM), lambda b, t: (b, 0, 0))
    return pl.pallas_call(
        _inproj_kernel,
        out_shape=[jax.ShapeDtypeStruct((B, T, w), F32) for w in widths]
        + [jax.ShapeDtypeStruct((B, CONV_W - 1, GDN_CONV_DIM), F32)],
        grid=(B // bb, T // tt),
        in_specs=[_tok_spec(bb, tt, d), _mod_spec(3, bb, d), _const_spec((1, d)),
                  _const_spec(w_main.shape), _const_spec(w_ab.shape), hist_spec,
                  _const_spec(conv_w.shape)],
        out_specs=[_tok_spec(bb, tt, w) for w in widths] + [hist_spec],
        scratch_shapes=[pltpu.VMEM((bb, _XP_PAD + tt, GDN_CONV_DIM), F32)],
        compiler_params=_params(("parallel", "arbitrary")),
        name="inproj",
    )(x, mod3, gain, w_main, w_ab, prefix, conv_w)


GDN_PACK = 4


def _pack_masks(size, pk):
    row = lax.broadcasted_iota(jnp.int32, (size, pk * size), 0)
    lane = lax.broadcasted_iota(jnp.int32, (size, pk * size), 1)
    col = lane % size
    merges = []
    s = 1
    while s < size:
        rb, cb = row // s, col // s
        merges.append((rb == cb + 1) & (rb % 2 == 1))
        s *= 2
    return dict(seg=lane // size, eye=row == col, incl=row >= col, strict=row > col, merges=merges)


def _select_seg(seg, parts):
    out = parts[-1]
    for k in range(len(parts) - 2, -1, -1):
        out = jnp.where(seg == k, parts[k], out)
    return out


def _block_diag_packed(m, seg, pk):
    zero = jnp.zeros_like(m)
    return jnp.concatenate([jnp.where(seg == k, m, zero) for k in range(pk)], axis=0)


def _block_diag(blocks):
    n = len(blocks)
    zero = jnp.zeros_like(blocks[0])
    return jnp.concatenate(
        [jnp.concatenate([blocks[k] if j == k else zero for j in range(n)], axis=1) for k in range(n)],
        axis=0)


def _gdn_kernel(conv_ref, ab_ref, s0_ref, alog_ref, dtb_ref, o_ref, sout_ref, s_scr, *, rb, cps):
    c = pl.program_id(1)
    L = CHUNK
    pk = GDN_PACK
    ngrp = GDN_HEADS // pk

    @pl.when(c == 0)
    def _():
        s_scr[...] = s0_ref[...]

    mk = _pack_masks(L, pk)
    seg, incl, strict, merges = mk["seg"], mk["incl"], mk["strict"], mk["merges"]
    row64 = lax.broadcasted_iota(jnp.int32, (L, L), 0)
    col64 = lax.broadcasted_iota(jnp.int32, (L, L), 1)
    incl64 = jnp.where(row64 >= col64, 1.0, 0.0).astype(BF16)
    incl3 = jnp.concatenate([incl64] * 3, axis=1)
    upper_p = jnp.where(strict, 0.0, 1.0).astype(BF16)
    upper3 = jnp.concatenate([upper_p] * 3, axis=0)

    chunks = [(r, i) for r in range(rb) for i in range(cps)]
    units = [(r, i, p) for (r, i) in chunks for p in range(ngrp)]
    heads = [(r, i, h) for (r, i) in chunks for h in range(GDN_HEADS)]

    def head_cols(off, r, i, h):
        return conv_ref[r, i * L:(i + 1) * L, off + h * GDN_DK:off + (h + 1) * GDN_DK]

    beta, gc, g2 = {}, {}, {}
    for (r, i) in chunks:
        ab = ab_ref[r, i * L:(i + 1) * L, :]
        g = -jnp.exp(alog_ref[...]) * _softplus(ab[:, :LANES] + dtb_ref[...])
        beta[r, i] = jax.nn.sigmoid(ab[:, LANES:])
        g_hi = g.astype(BF16)
        g_mid, g_lo = _split_bf16(g - g_hi.astype(F32))
        g3 = jnp.concatenate([g_hi, g_mid, g_lo], axis=0)
        gc[r, i] = _dot(incl3, g3)
        g2[r, i] = _dot_tn(g3, upper3)

    qn, kn, vb, qd, kdec, dl, gcb, btb = {}, {}, {}, {}, {}, {}, {}, {}
    for (r, i, h) in heads:
        qh, kh = head_cols(0, r, i, h), head_cols(GDN_QK_DIM, r, i, h)
        qn[r, i, h] = qh * lax.rsqrt(jnp.sum(qh * qh, axis=-1, keepdims=True) + EPS) * (GDN_DK ** -0.5)
        kn[r, i, h] = kh * lax.rsqrt(jnp.sum(kh * kh, axis=-1, keepdims=True) + EPS)
        gcb[r, i, h] = jnp.broadcast_to(gc[r, i][:, h:h + 1], (L, LANES))
        btb[r, i, h] = jnp.broadcast_to(beta[r, i][:, h:h + 1], (L, LANES))
        egb = jnp.exp(gcb[r, i, h])
        glb = gcb[r, i, h][L - 1:L, :]
        vb[r, i, h] = jnp.concatenate([head_cols(2 * GDN_QK_DIM, r, i, h) * btb[r, i, h],
                                       kn[r, i, h] * (btb[r, i, h] * egb)], axis=-1).astype(BF16)
        qd[r, i, h] = qn[r, i, h] * egb
        kdec[r, i, h] = (kn[r, i, h] * jnp.exp(glb - gcb[r, i, h])).astype(BF16)
        dl[r, i, h] = jnp.exp(glb)

    def packed(cols):
        reps = (pk * L) // LANES
        return _select_seg(seg, [jnp.concatenate([cc] * reps, axis=1) for cc in cols])

    gamma, n_p, qkg = {}, {}, {}
    for (r, i, p) in units:
        hs = range(pk * p, pk * (p + 1))
        gr_p = _select_seg(seg[0:1], [g2[r, i][h:h + 1, :] for h in hs])
        gamma[r, i, p] = jnp.exp(jnp.where(incl, packed([gcb[r, i, h] for h in hs]) - gr_p, -jnp.inf))
    for (r, i, p) in units:
        hs = range(pk * p, pk * (p + 1))
        kb = [kn[r, i, h].astype(BF16) for h in hs]
        lhs = jnp.concatenate([jnp.concatenate(kb, axis=1),
                               jnp.concatenate([qn[r, i, h].astype(BF16) for h in hs], axis=1)],
                              axis=0)
        res = _dot_nt(lhs, _block_diag(kb))
        bt_p = packed([btb[r, i, h] for h in hs])
        n_p[r, i, p] = jnp.where(strict, bt_p * res[:L] * gamma[r, i, p], 0.0)
        qkg[r, i, p] = (res[L:] * gamma[r, i, p]).astype(BF16)

    t = {u: jnp.where(mk["eye"], 1.0, 0.0) - jnp.where(merges[0], n_p[u], 0.0) for u in units}
    for m in merges[1:]:
        ct = {u: _dot(jnp.where(m, n_p[u], 0.0).astype(BF16), _block_diag_packed(t[u].astype(BF16), seg, pk))
              for u in units}
        t = {u: t[u] - _dot(t[u].astype(BF16), _block_diag_packed(ct[u].astype(BF16), seg, pk))
             for u in units}

    u_h, w_h = {}, {}
    for (r, i, p) in units:
        hs = range(pk * p, pk * (p + 1))
        uw = _dot(t[r, i, p].astype(BF16), _block_diag([vb[r, i, h] for h in hs]))
        for k, h in enumerate(hs):
            u_h[r, i, h] = uw[:, 2 * k * GDN_DV:(2 * k + 1) * GDN_DV]
            w_h[r, i, h] = uw[:, (2 * k + 1) * GDN_DV:(2 * k + 2) * GDN_DV]

    pairs = [(a, a + 1) for a in range(0, GDN_HEADS, 2)]
    wq = {}
    for (r, i) in chunks:
        for (a, b) in pairs:
            wq[r, i, a] = jnp.concatenate(
                [jnp.concatenate([w_h[r, i, a], w_h[r, i, b]], axis=1),
                 jnp.concatenate([qd[r, i, a], qd[r, i, b]], axis=1)],
                axis=0).astype(BF16)
    s_cur = {(r, h): s_scr[r, h] for r in range(rb) for h in range(GDN_HEADS)}
    for i in range(cps):
        rs, vn = {}, {}
        for r in range(rb):
            for (a, b) in pairs:
                rs[r, a] = _dot(wq[r, i, a], _block_diag([s_cur[r, a].astype(BF16), s_cur[r, b].astype(BF16)]))
        for r in range(rb):
            for (a, b) in pairs:
                vn[r, a] = (u_h[r, i, a] - rs[r, a][:L, :GDN_DV]).astype(BF16)
                vn[r, b] = (u_h[r, i, b] - rs[r, a][:L, GDN_DV:]).astype(BF16)
        for r in range(rb):
            for (a, b) in pairs:
                lo = (a % pk) * L
                o2 = _dot(qkg[r, i, a // pk][:, lo:lo + 2 * L], _block_diag([vn[r, a], vn[r, b]]))
                o_ref[r, i * L:(i + 1) * L, a * GDN_DV:(a + 2) * GDN_DV] = rs[r, a][L:, :] + o2
        for r in range(rb):
            for h in range(GDN_HEADS):
                s_cur[r, h] = dl[r, i, h] * s_cur[r, h] + _dot_tn(kdec[r, i, h], vn[r, h])

    for r in range(rb):
        for h in range(GDN_HEADS):
            s_scr[r, h] = s_cur[r, h]

    @pl.when(c == pl.num_programs(1) - 1)
    def _():
        sout_ref[...] = s_scr[...]


def _gdn(conv_act, ab, s0, alog_pad, dtb_pad, *, rb, cps):
    B, T, cd = conv_act.shape
    tg = cps * CHUNK
    state_spec = pl.BlockSpec((rb,) + s0.shape[1:], lambda b, c: (b, 0, 0, 0))
    return pl.pallas_call(
        functools.partial(_gdn_kernel, rb=rb, cps=cps),
        out_shape=[jax.ShapeDtypeStruct((B, T, GDN_V_DIM), F32),
                   jax.ShapeDtypeStruct(s0.shape, F32)],
        grid=(B // rb, T // tg),
        in_specs=[pl.BlockSpec((rb, tg, cd), lambda b, c: (b, c, 0)),
                  pl.BlockSpec((rb, tg, _AB_WIDTH), lambda b, c: (b, c, 0)),
                  state_spec, _const_spec((1, LANES)), _const_spec((1, LANES))],
        out_specs=[pl.BlockSpec((rb, tg, GDN_V_DIM), lambda b, c: (b, c, 0)), state_spec],
        scratch_shapes=[pltpu.VMEM((rb,) + s0.shape[1:], F32)],
        compiler_params=_params(("parallel", "arbitrary")),
        name="gdn",
    )(conv_act, ab, s0, alog_pad, dtb_pad)


def _segment_rms(x, gain, ones2):
    hi, lo = _split_bf16(x * x)
    ss = _dot(jnp.concatenate([hi, lo], axis=1), ones2)
    return x * lax.rsqrt(ss * (1.0 / SWA_HD) + EPS) * gain


def _swa_kernel(q_ref, kv_ref, kc_ref, vc_ref, qg_ref, kg_ref, sink_ref,
                o_ref, kout_ref, vout_ref, kwin, vwin, *, rb, tq, mask_history):
    c = pl.program_id(1)
    L = CHUNK
    span = WINDOW + L
    kvd = SWA_KV_DIM
    nj = tq // L

    @pl.when(c == 0)
    def _():
        kwin[:, 0:WINDOW, :] = kc_ref[...]
        vwin[:, 0:WINDOW, :] = vc_ref[...]

    seg_r = lax.broadcasted_iota(jnp.int32, (kvd, kvd), 0) // SWA_HD
    seg_c = lax.broadcasted_iota(jnp.int32, (kvd, kvd), 1) // SWA_HD
    ones_bd = jnp.where(seg_r == seg_c, 1.0, 0.0).astype(BF16)
    ones2 = jnp.concatenate([ones_bd, ones_bd], axis=0)

    rows = SWA_GROUP * L
    scale = SWA_HD ** -0.5
    qgain = qg_ref[...] * scale
    sink_all = jnp.concatenate([jnp.broadcast_to(sink_ref[:, h:h + 1], (L, LANES)) for h in range(SWA_HQ)],
                               axis=0)
    seg_o = lax.broadcasted_iota(jnp.int32, (rows, kvd), 1) // SWA_HD
    seg_w = lax.broadcasted_iota(jnp.int32, (WINDOW + tq, kvd), 1) // SWA_HD

    for r in range(rb):
        kv = kv_ref[r]
        kwin[r, WINDOW:WINDOW + tq, :] = _segment_rms(kv[:, :kvd], kg_ref[...], ones2)
        vwin[r, WINDOW:WINDOW + tq, :] = kv[:, kvd:]

    k_only, v_only, qn = {}, {}, {}
    for r in range(rb):
        kw, vw = kwin[r], vwin[r]
        for hk in range(SWA_HKV):
            k_only[r, hk] = jnp.where(seg_w == hk, kw, 0.0).astype(BF16)
            v_only[r, hk] = jnp.where(seg_w == hk, vw, 0.0).astype(BF16)
        for j in range(nj):
            q_stack = jnp.concatenate([q_ref[r, j * L:(j + 1) * L, g * kvd:(g + 1) * kvd]
                                       for g in range(SWA_GROUP)], axis=0)
            qn[r, j] = _segment_rms(q_stack, qgain, ones2).astype(BF16)

    def scores(r, j):
        return [_dot_nt(qn[r, j], k_only[r, hk][j * L:j * L + span]) for hk in range(SWA_HKV)]

    steps = [(r, j) for r in range(rb) for j in range(nj)]
    s_next = scores(*steps[0])
    for n, (r, j) in enumerate(steps):
        s_all = jnp.concatenate(s_next, axis=0)
        if n + 1 < len(steps):
            s_next = scores(*steps[n + 1])
        if mask_history:
            kpos = c * tq + (j * L - WINDOW) + lax.broadcasted_iota(jnp.int32, s_all.shape, 1)
            s_all = jnp.where(kpos >= 0, s_all, -jnp.inf)
        m = jnp.maximum(jnp.max(s_all, axis=-1, keepdims=True), sink_all)
        p = jnp.exp(s_all - jnp.concatenate([m, m[:, :span - LANES]], axis=1))
        den = jnp.sum(p, axis=-1, keepdims=True) + jnp.exp(sink_all - m)
        pb = p.astype(BF16)
        acc = None
        den_full = None
        for hk in range(SWA_HKV):
            pv = _dot(pb[hk * rows:(hk + 1) * rows], v_only[r, hk][j * L:j * L + span])
            acc = pv if acc is None else acc + pv
            dh = jnp.concatenate([den[hk * rows:(hk + 1) * rows]] * (kvd // LANES), axis=1)
            den_full = dh if den_full is None else jnp.where(seg_o == hk, dh, den_full)
        o = acc / den_full
        for g in range(SWA_GROUP):
            o_ref[r, j * L:(j + 1) * L, g * kvd:(g + 1) * kvd] = o[g * L:(g + 1) * L]

    knew = kwin[:, tq:tq + WINDOW, :]
    vnew = vwin[:, tq:tq + WINDOW, :]
    kwin[:, 0:WINDOW, :] = knew
    vwin[:, 0:WINDOW, :] = vnew

    @pl.when(c == pl.num_programs(1) - 1)
    def _():
        kout_ref[...] = knew
        vout_ref[...] = vnew


def _swa(q, kv, k_cache, v_cache, q_gain, k_gain, sinks, *, rb, tq, mask_history):
    B, T, _ = q.shape
    tok = lambda w: pl.BlockSpec((rb, tq, w), lambda b, c: (b, c, 0))
    win = pl.BlockSpec((rb, WINDOW, SWA_KV_DIM), lambda b, c: (b, 0, 0))
    return pl.pallas_call(
        functools.partial(_swa_kernel, rb=rb, tq=tq, mask_history=mask_history),
        out_shape=[jax.ShapeDtypeStruct((B, T, SWA_Q_DIM), F32),
                   jax.ShapeDtypeStruct((B, WINDOW, SWA_KV_DIM), F32),
                   jax.ShapeDtypeStruct((B, WINDOW, SWA_KV_DIM), F32)],
        grid=(B // rb, T // tq),
        in_specs=[tok(SWA_Q_DIM), tok(2 * SWA_KV_DIM), win, win,
                  _const_spec((1, SWA_KV_DIM)), _const_spec((1, SWA_KV_DIM)), _const_spec((1, SWA_HQ))],
        out_specs=[tok(SWA_Q_DIM), win, win],
        scratch_shapes=[pltpu.VMEM((rb, WINDOW + tq, SWA_KV_DIM), F32),
                        pltpu.VMEM((rb, WINDOW + tq, SWA_KV_DIM), F32)],
        compiler_params=_params(("parallel", "arbitrary")),
        name="swa",
    )(q, kv, k_cache, v_cache, q_gain, k_gain, sinks)


_FFN_ROWS = 512
_PROJ_ROWS = 256
_GDN_CHUNK_ROWS = 8
_SWA_QUERIES = 512


def _tile_plan(batch, seq):
    def rows_by_time(target):
        tt = min(seq, target)
        return max(1, min(batch, target // tt)), tt
    bb, tt = rows_by_time(_FFN_ROWS)
    pbb, ptt = rows_by_time(_PROJ_ROWS)
    cps = min(seq // CHUNK, 4)
    rb = max(1, min(batch, _GDN_CHUNK_ROWS // cps))
    srb, tq = rows_by_time(_SWA_QUERIES)
    srb = min(srb, 4)
    return dict(bb=bb, tt=tt, pbb=pbb, ptt=ptt, rb=rb, cps=cps, srb=srb, tq=tq)


def _layer(x, mod, conv_prefix, s0, k_cache, v_cache, wts, *, mask_history):
    bb, tt, pbb, ptt, rb, cps, srb, tq = (_tile_plan(x.shape[0], x.shape[1])[k] for k in
                                            ("bb", "tt", "pbb", "ptt", "rb", "cps", "srb", "tq"))
    (norm_ffn1, ffn1_in, ffn1_out, norm_mix, w_main, w_ab, conv_w, alog_pad, dtb_pad, gnorm,
     q_gain, k_gain, sinks, b_merge, w_o, norm_ffn2, ffn2_in, ffn2_out, tf) = wts
    B, T, d = x.shape
    x = _ffn(x, mod[0:3], norm_ffn1, ffn1_in, ffn1_out, bb=bb, tt=tt, tf=tf)
    conv_act, z, q_s, kv_s, gates, ab, new_conv = _inproj(x, mod[3:6], norm_mix, w_main, w_ab,
                                                          conv_prefix, conv_w, bb=pbb, tt=ptt)
    o_g, new_s = _gdn(conv_act, ab, s0, alog_pad, dtb_pad, rb=rb, cps=cps)
    o_s, new_k, new_v = _swa(q_s, kv_s, k_cache, v_cache, q_gain, k_gain, sinks,
                             rb=srb, tq=tq, mask_history=mask_history)
    x = _mix_ffn(x, o_g, z, gnorm, o_s, gates, b_merge, w_o, mod[3:6], mod[6:9], norm_ffn2, ffn2_in,
                 ffn2_out, bb=bb, tt=tt, tf=tf)
    new_k = new_k.reshape(B, WINDOW, SWA_HKV, SWA_HD)
    new_v = new_v.reshape(B, WINDOW, SWA_HKV, SWA_HD)
    return x, new_conv, new_s, new_k, new_v


def _group_major(w, axis):
    shape = w.shape
    split = shape[:axis] + (SWA_HKV, SWA_GROUP, SWA_HD) + shape[axis + 1:]
    return jnp.swapaxes(w.reshape(split), axis, axis + 1).reshape(shape)


def _pad_lanes(v):
    return jnp.pad(v.astype(F32), (0, LANES - v.shape[0])).reshape(1, LANES)


def kernel(x_prompt, x_sample, state_gdn_conv, state_gdn, cache_swa_k, cache_swa_v, c_prompt, c_sample, w_ada, b_ada, norm_ffn1, ffn1_w_in, ffn1_w_out, norm_mix, w_in, gdn_conv_w, gdn_a_log, gdn_dt_bias, gdn_norm, swa_q_norm, swa_k_norm, swa_sinks, b_merge, w_out, norm_ffn2, ffn2_w_in, ffn2_w_out):
    depth = w_ada.shape[0]
    bp, tp, d = x_prompt.shape
    bs, ts, _ = x_sample.shape
    tf = 256
    yp, ys = x_prompt, x_sample
    outs_p, outs_s = [], []
    for l in range(depth):
        mod = _adaln(jnp.concatenate([c_prompt, c_sample], axis=0), w_ada[l], b_ada[l])
        mod = mod.reshape(N_MOD, bp + bs, 1, d)
        wl = w_in[l]
        o_conv, o_z = GDN_CONV_DIM, GDN_CONV_DIM + GDN_V_DIM
        o_a, o_b = o_z, o_z + GDN_HEADS
        o_q = o_b + GDN_HEADS
        o_k = o_q + SWA_Q_DIM
        o_g = o_k + 2 * SWA_KV_DIM
        o_gb = o_g + d
        w_main = jnp.concatenate(
            [wl[:, :o_z], _group_major(wl[:, o_q:o_k], 1), wl[:, o_k:o_gb], _group_major(wl[:, o_gb:], 1)],
            axis=1).astype(BF16)
        pad = jnp.zeros((d, LANES - GDN_HEADS), wl.dtype)
        w_ab = jnp.concatenate([wl[:, o_a:o_b], pad, wl[:, o_b:o_q], pad], axis=1).astype(BF16)
        bm = jnp.concatenate([b_merge[l][:d], _group_major(b_merge[l][d:], 0)]).reshape(1, 2 * d)
        w_o2 = jnp.concatenate([w_out[l], _group_major(w_out[l], 0)], axis=0).astype(BF16)
        wts = (norm_ffn1[l].reshape(1, d), ffn1_w_in[l].astype(BF16), ffn1_w_out[l].astype(BF16),
               norm_mix[l].reshape(1, d), w_main, w_ab, gdn_conv_w[l],
               _pad_lanes(gdn_a_log[l]), _pad_lanes(gdn_dt_bias[l]), gdn_norm[l].reshape(1, GDN_DV),
               jnp.tile(swa_q_norm[l], SWA_HKV).reshape(1, SWA_KV_DIM),
               jnp.tile(swa_k_norm[l], SWA_HKV).reshape(1, SWA_KV_DIM),
               swa_sinks[l].reshape(1, SWA_HQ).astype(F32), bm,
               w_o2, norm_ffn2[l].reshape(1, d),
               ffn2_w_in[l].astype(BF16), ffn2_w_out[l].astype(BF16), tf)
        zero_conv = jnp.zeros((bp, CONV_W - 1, GDN_CONV_DIM), F32)
        zero_s = jnp.zeros((bp, GDN_HEADS, GDN_DK, GDN_DV), F32)
        zero_kv = jnp.zeros((bp, WINDOW, SWA_KV_DIM), F32)
        yp, *rest_p = _layer(yp, mod[:, :bp], zero_conv, zero_s, zero_kv, zero_kv, wts, mask_history=True)
        ys, *rest_s = _layer(ys, mod[:, bp:], state_gdn_conv[l], state_gdn[l],
                             cache_swa_k[l].reshape(bs, WINDOW, SWA_KV_DIM),
                             cache_swa_v[l].reshape(bs, WINDOW, SWA_KV_DIM), wts, mask_history=False)
        outs_p.append(rest_p)
        outs_s.append(rest_s)
    stack = lambda outs, i: jnp.stack([o[i] for o in outs])
    return (yp, ys,
            stack(outs_p, 0), stack(outs_p, 1), stack(outs_p, 2), stack(outs_p, 3),
            stack(outs_s, 0), stack(outs_s, 1), stack(outs_s, 2), stack(outs_s, 3))
```

```python
import functools

import jax
import jax.numpy as jnp
from jax import lax
from jax.experimental import pallas as pl
from jax.experimental.pallas import tpu as pltpu

F32 = jnp.float32
BF16 = jnp.bfloat16

CHUNK = 64
GDN_HEADS = 8
GDN_DK = 128
GDN_DV = 128
GDN_QK_DIM = GDN_HEADS * GDN_DK
GDN_V_DIM = GDN_HEADS * GDN_DV
GDN_CONV_DIM = 2 * GDN_QK_DIM + GDN_V_DIM
CONV_W = 4
SWA_HQ = 16
SWA_HKV = 4
SWA_HD = 64
SWA_GROUP = SWA_HQ // SWA_HKV
SWA_Q_DIM = SWA_HQ * SWA_HD
SWA_KV_DIM = SWA_HKV * SWA_HD
WINDOW = 128
N_MOD = 9
EPS = 1e-6
LANES = 128
VMEM_LIMIT_BYTES = 60 * 1024 * 1024


def _dot(a, b):
    return jnp.dot(a, b, preferred_element_type=F32)


def _dot_nt(a, b):
    return lax.dot_general(a, b, (((1,), (1,)), ((), ())), preferred_element_type=F32)


def _dot_tn(a, b, precision=None):
    return lax.dot_general(a, b, (((0,), (0,)), ((), ())), preferred_element_type=F32,
                           precision=precision)


def _split_bf16(x):
    hi = x.astype(BF16)
    return hi, (x - hi.astype(F32)).astype(BF16)


def _silu(x):
    return x * jax.nn.sigmoid(x)


def _softplus(x):
    return jnp.maximum(x, 0.0) + jnp.log1p(jnp.exp(-jnp.abs(x)))


def _modulate(x, gain, shift, scale):
    y = x * lax.rsqrt(jnp.mean(x * x, axis=-1, keepdims=True) + EPS)
    return (y * gain) * (1 + scale) + shift


def _const_spec(shape):
    nd = len(shape)
    return pl.BlockSpec(shape, lambda *_: (0,) * nd, pipeline_mode=pl.Buffered(1))


def _params(semantics):
    return pltpu.CompilerParams(dimension_semantics=semantics, vmem_limit_bytes=VMEM_LIMIT_BYTES)


def _adaln_kernel(c_ref, w_ref, b_ref, o_ref):
    s = _silu(c_ref[...]).astype(BF16)
    o_ref[0] = _dot(s, w_ref[...].astype(BF16)) + b_ref[0]


def _adaln(c, w_ada, b_ada):
    nb, d = c.shape
    return pl.pallas_call(
        _adaln_kernel,
        out_shape=jax.ShapeDtypeStruct((N_MOD, nb, d), F32),
        grid=(N_MOD,),
        in_specs=[pl.BlockSpec((nb, d), lambda j: (0, 0)),
                  pl.BlockSpec((d, d), lambda j: (0, j)),
                  pl.BlockSpec((1, 1, d), lambda j: (j, 0, 0))],
        out_specs=pl.BlockSpec((1, nb, d), lambda j: (j, 0, 0)),
        compiler_params=_params(("parallel",)),
        name="adaln",
    )(c, w_ada, b_ada.reshape(N_MOD, 1, d))


def _ffn_body(x, mod_ref, gain_ref, win_ref, wout_ref, act_ref, tf):
    bb, tt, d = x.shape
    d_ff = wout_ref.shape[0]
    sh, sc, gt = mod_ref[0], mod_ref[1], mod_ref[2]
    h = _modulate(x, gain_ref[...], sh, sc).reshape(bb * tt, d).astype(BF16)
    for j in range(d_ff // tf):
        gate = _dot(h, win_ref[:, j * tf:(j + 1) * tf])
        up = _dot(h, win_ref[:, d_ff + j * tf:d_ff + (j + 1) * tf])
        act_ref[:, j * tf:(j + 1) * tf] = (_silu(gate) * up).astype(BF16)
    y = _dot(act_ref[...], wout_ref[...])
    return x + (0.5 * gt) * y.reshape(bb, tt, d)


def _ffn_kernel(x_ref, mod_ref, gain_ref, win_ref, wout_ref, o_ref, act_ref, *, tf):
    o_ref[...] = _ffn_body(x_ref[...], mod_ref, gain_ref, win_ref, wout_ref, act_ref, tf)


def _mix_ffn_kernel(x_ref, og_ref, z_ref, gnorm_ref, os_ref, gates_ref, bm_ref, wo_ref, gt2_ref,
                    mod_ref, gain_ref, win_ref, wout_ref, o_ref, act_ref, *, tf):
    x = x_ref[...]
    bb, tt, d = x.shape
    gnorm = gnorm_ref[...]
    og_heads = []
    for h in range(GDN_HEADS):
        o = og_ref[:, :, h * GDN_DV:(h + 1) * GDN_DV]
        on = o * lax.rsqrt(jnp.mean(o * o, axis=-1, keepdims=True) + EPS) * gnorm
        og_heads.append(on * _silu(z_ref[:, :, h * GDN_DV:(h + 1) * GDN_DV]))
    gl = jax.nn.sigmoid(gates_ref[...] + bm_ref[...])
    mix = gl * jnp.concatenate(og_heads + [os_ref[...]], axis=-1)
    y = _dot(mix.reshape(bb * tt, 2 * d).astype(BF16), wo_ref[...]).reshape(bb, tt, d)
    x1 = x + gt2_ref[0] * y
    o_ref[...] = _ffn_body(x1, mod_ref, gain_ref, win_ref, wout_ref, act_ref, tf)


def _tok_spec(bb, tt, width):
    return pl.BlockSpec((bb, tt, width), lambda b, t: (b, t, 0))


def _mod_spec(n, bb, d):
    return pl.BlockSpec((n, bb, 1, d), lambda b, t: (0, b, 0, 0))


def _ffn(x, mod3, gain, w_in_bf16, w_out, *, bb, tt, tf):
    B, T, d = x.shape
    d_ff = w_out.shape[0]
    assert d_ff % tf == 0 and w_in_bf16.shape == (d, 2 * d_ff), (w_in_bf16.shape, tf)
    return pl.pallas_call(
        functools.partial(_ffn_kernel, tf=tf),
        out_shape=jax.ShapeDtypeStruct(x.shape, F32),
        grid=(B // bb, T // tt),
        in_specs=[_tok_spec(bb, tt, d), _mod_spec(3, bb, d), _const_spec((1, d)),
                  _const_spec((d, 2 * d_ff)), _const_spec((d_ff, d))],
        out_specs=_tok_spec(bb, tt, d),
        scratch_shapes=[pltpu.VMEM((bb * tt, d_ff), BF16)],
        compiler_params=_params(("parallel", "parallel")),
        name="ffn",
    )(x, mod3, gain, w_in_bf16, w_out)


def _mix_ffn(x, og, z, gnorm, os_, gates, b_merge, w_o, mod_mix, mod3, gain, w_in_bf16, w_out,
             *, bb, tt, tf):
    B, T, d = x.shape
    d_ff = w_out.shape[0]
    assert d_ff % tf == 0 and w_in_bf16.shape == (d, 2 * d_ff), (w_in_bf16.shape, tf)
    return pl.pallas_call(
        functools.partial(_mix_ffn_kernel, tf=tf),
        out_shape=jax.ShapeDtypeStruct(x.shape, F32),
        grid=(B // bb, T // tt),
        in_specs=[_tok_spec(bb, tt, d), _tok_spec(bb, tt, d), _tok_spec(bb, tt, d),
                  _const_spec((1, GDN_DV)), _tok_spec(bb, tt, d),
                  _tok_spec(bb, tt, 2 * d), _const_spec((1, 2 * d)), _const_spec((2 * d, d)),
                  pl.BlockSpec((1, bb, 1, d), lambda b, t: (2, b, 0, 0)),
                  _mod_spec(3, bb, d), _const_spec((1, d)),
                  _const_spec((d, 2 * d_ff)), _const_spec((d_ff, d))],
        out_specs=_tok_spec(bb, tt, d),
        scratch_shapes=[pltpu.VMEM((bb * tt, d_ff), BF16)],
        compiler_params=_params(("parallel", "parallel")),
        name="mix_ffn",
    )(x, og, z, gnorm, os_, gates, b_merge, w_o, mod_mix, mod3, gain, w_in_bf16, w_out)


_PROJ_STEP = 512
_AB_WIDTH = 2 * LANES
_XP_PAD = 8


def _inproj_kernel(x_ref, mod_ref, gain_ref, w_ref, wab_ref, prefix_ref, convw_ref,
                   conv_ref, z_ref, q_ref, kv_ref, gates_ref, ab_ref, convout_ref, xp_scr):
    t_idx = pl.program_id(1)
    x = x_ref[...]
    bb, tt, d = x.shape
    hist = CONV_W - 1

    @pl.when(t_idx == 0)
    def _():
        xp_scr[:, _XP_PAD - hist:_XP_PAD, :] = prefix_ref[...]

    h = _modulate(x, gain_ref[...], mod_ref[0], mod_ref[1]).reshape(bb * tt, d).astype(BF16)
    cw = convw_ref[...]
    for lo in range(0, GDN_CONV_DIM, _PROJ_STEP):
        hi = lo + _PROJ_STEP
        pre = _dot(h, w_ref[:, lo:hi]).reshape(bb, tt, hi - lo)
        xp_scr[:, _XP_PAD:_XP_PAD + tt, lo:hi] = pre
        y = pre * cw[hist:hist + 1, lo:hi]
        for i in range(hist):
            y = y + xp_scr[:, _XP_PAD - hist + i:_XP_PAD - hist + i + tt, lo:hi] * cw[i:i + 1, lo:hi]
        conv_ref[:, :, lo:hi] = _silu(y)
    tail = xp_scr[:, tt:tt + _XP_PAD, :]
    xp_scr[:, 0:_XP_PAD, :] = tail

    col = GDN_CONV_DIM
    for ref in (z_ref, q_ref, kv_ref, gates_ref):
        width = ref.shape[-1]
        for lo in range(0, width, _PROJ_STEP):
            hi = min(lo + _PROJ_STEP, width)
            ref[:, :, lo:hi] = _dot(h, w_ref[:, col + lo:col + hi]).reshape(bb, tt, hi - lo)
        col += width
    ab_ref[...] = _dot(h, wab_ref[...]).reshape(bb, tt, _AB_WIDTH)

    @pl.when(t_idx == pl.num_programs(1) - 1)
    def _():
        convout_ref[...] = tail[:, _XP_PAD - hist:, :]


def _inproj(x, mod3, gain, w_main, w_ab, prefix, conv_w, *, bb, tt):
    B, T, d = x.shape
    widths = (GDN_CONV_DIM, GDN_V_DIM, SWA_Q_DIM, 2 * SWA_KV_DIM, 2 * d, _AB_WIDTH)
    hist_spec = pl.BlockSpec((bb, CONV_W - 1, GDN_CONV_DIM), lambda b, t: (b, 0, 0))
    return pl.pallas_call(
        _inproj_kernel,
        out_shape=[jax.ShapeDtypeStruct((B, T, w), F32) for w in widths]
        + [jax.ShapeDtypeStruct((B, CONV_W - 1, GDN_CONV_DIM), F32)],
        grid=(B // bb, T // tt),
        in_specs=[_tok_spec(bb, tt, d), _mod_spec(3, bb, d), _const_spec((1, d)),
                  _const_spec(w_main.shape), _const_spec(w_ab.shape), hist_spec,
                  _const_spec(conv_w.shape)],
        out_specs=[_tok_spec(bb, tt, w) for w in widths] + [hist_spec],
        scratch_shapes=[pltpu.VMEM((bb, _XP_PAD + tt, GDN_CONV_DIM), F32)],
        compiler_params=_params(("parallel", "arbitrary")),
        name="inproj",
    )(x, mod3, gain, w_main, w_ab, prefix, conv_w)


GDN_PACK = 4


def _pack_masks(size, pk):
    row = lax.broadcasted_iota(jnp.int32, (size, pk * size), 0)
    lane = lax.broadcasted_iota(jnp.int32, (size, pk * size), 1)
    col = lane % size
    merges = []
    s = 1
    while s < size:
        rb, cb = row // s, col // s
        merges.append((rb == cb + 1) & (rb % 2 == 1))
        s *= 2
    return dict(seg=lane // size, eye=row == col, incl=row >= col, strict=row > col, merges=merges)


def _select_seg(seg, parts):
    out = parts[-1]
    for k in range(len(parts) - 2, -1, -1):
        out = jnp.where(seg == k, parts[k], out)
    return out


def _block_diag_packed(m, seg, pk):
    zero = jnp.zeros_like(m)
    return jnp.concatenate([jnp.where(seg == k, m, zero) for k in range(pk)], axis=0)


def _block_diag(blocks):
    n = len(blocks)
    zero = jnp.zeros_like(blocks[0])
    return jnp.concatenate(
        [jnp.concatenate([blocks[k] if j == k else zero for j in range(n)], axis=1) for k in range(n)],
        axis=0)


def _gdn_kernel(conv_ref, ab_ref, s0_ref, alog_ref, dtb_ref, o_ref, sout_ref, s_scr, *, rb, cps):
    c = pl.program_id(1)
    L = CHUNK
    pk = GDN_PACK
    ngrp = GDN_HEADS // pk

    @pl.when(c == 0)
    def _():
        s_scr[...] = s0_ref[...]

    mk = _pack_masks(L, pk)
    seg, incl, strict, merges = mk["seg"], mk["incl"], mk["strict"], mk["merges"]
    row64 = lax.broadcasted_iota(jnp.int32, (L, L), 0)
    col64 = lax.broadcasted_iota(jnp.int32, (L, L), 1)
    incl64 = jnp.where(row64 >= col64, 1.0, 0.0).astype(BF16)
    incl3 = jnp.concatenate([incl64] * 3, axis=1)
    upper_p = jnp.where(strict, 0.0, 1.0).astype(BF16)
    upper3 = jnp.concatenate([upper_p] * 3, axis=0)

    chunks = [(r, i) for r in range(rb) for i in range(cps)]
    units = [(r, i, p) for (r, i) in chunks for p in range(ngrp)]
    heads = [(r, i, h) for (r, i) in chunks for h in range(GDN_HEADS)]

    def head_cols(off, r, i, h):
        return conv_ref[r, i * L:(i + 1) * L, off + h * GDN_DK:off + (h + 1) * GDN_DK]

    beta, gc, g2 = {}, {}, {}
    for (r, i) in chunks:
        ab = ab_ref[r, i * L:(i + 1) * L, :]
        g = -jnp.exp(alog_ref[...]) * _softplus(ab[:, :LANES] + dtb_ref[...])
        beta[r, i] = jax.nn.sigmoid(ab[:, LANES:])
        g_hi = g.astype(BF16)
        g_mid, g_lo = _split_bf16(g - g_hi.astype(F32))
        g3 = jnp.concatenate([g_hi, g_mid, g_lo], axis=0)
        gc[r, i] = _dot(incl3, g3)
        g2[r, i] = _dot_tn(g3, upper3)

    qn, kn, vb, qd, kdec, dl, gcb, btb = {}, {}, {}, {}, {}, {}, {}, {}
    for (r, i, h) in heads:
        qh, kh = head_cols(0, r, i, h), head_cols(GDN_QK_DIM, r, i, h)
        qn[r, i, h] = qh * lax.rsqrt(jnp.sum(qh * qh, axis=-1, keepdims=True) + EPS) * (GDN_DK ** -0.5)
        kn[r, i, h] = kh * lax.rsqrt(jnp.sum(kh * kh, axis=-1, keepdims=True) + EPS)
        gcb[r, i, h] = jnp.broadcast_to(gc[r, i][:, h:h + 1], (L, LANES))
        btb[r, i, h] = jnp.broadcast_to(beta[r, i][:, h:h + 1], (L, LANES))
        egb = jnp.exp(gcb[r, i, h])
        glb = gcb[r, i, h][L - 1:L, :]
        vb[r, i, h] = jnp.concatenate([head_cols(2 * GDN_QK_DIM, r, i, h) * btb[r, i, h],
                                       kn[r, i, h] * (btb[r, i, h] * egb)], axis=-1).astype(BF16)
        qd[r, i, h] = qn[r, i, h] * egb
        kdec[r, i, h] = (kn[r, i, h] * jnp.exp(glb - gcb[r, i, h])).astype(BF16)
        dl[r, i, h] = jnp.exp(glb)

    def packed(cols):
        reps = (pk * L) // LANES
        return _select_seg(seg, [jnp.concatenate([cc] * reps, axis=1) for cc in cols])

    gamma, n_p, qkg = {}, {}, {}
    for (r, i, p) in units:
        hs = range(pk * p, pk * (p + 1))
        gr_p = _select_seg(seg[0:1], [g2[r, i][h:h + 1, :] for h in hs])
        gamma[r, i, p] = jnp.exp(jnp.where(incl, packed([gcb[r, i, h] for h in hs]) - gr_p, -jnp.inf))
    for (r, i, p) in units:
        hs = range(pk * p, pk * (p + 1))
        kb = [kn[r, i, h].astype(BF16) for h in hs]
        lhs = jnp.concatenate([jnp.concatenate(kb, axis=1),
                               jnp.concatenate([qn[r, i, h].astype(BF16) for h in hs], axis=1)],
                              axis=0)
        res = _dot_nt(lhs, _block_diag(kb))
        bt_p = packed([btb[r, i, h] for h in hs])
        n_p[r, i, p] = jnp.where(strict, bt_p * res[:L] * gamma[r, i, p], 0.0)
        qkg[r, i, p] = (res[L:] * gamma[r, i, p]).astype(BF16)

    t = {u: jnp.where(mk["eye"], 1.0, 0.0) - jnp.where(merges[0], n_p[u], 0.0) for u in units}
    for m in merges[1:]:
        ct = {u: _dot(jnp.where(m, n_p[u], 0.0).astype(BF16), _block_diag_packed(t[u].astype(BF16), seg, pk))
              for u in units}
        t = {u: t[u] - _dot(t[u].astype(BF16), _block_diag_packed(ct[u].astype(BF16), seg, pk))
             for u in units}

    u_h, w_h = {}, {}
    for (r, i, p) in units:
        hs = range(pk * p, pk * (p + 1))
        uw = _dot(t[r, i, p].astype(BF16), _block_diag([vb[r, i, h] for h in hs]))
        for k, h in enumerate(hs):
            u_h[r, i, h] = uw[:, 2 * k * GDN_DV:(2 * k + 1) * GDN_DV]
            w_h[r, i, h] = uw[:, (2 * k + 1) * GDN_DV:(2 * k + 2) * GDN_DV]

    pairs = [(a, a + 1) for a in range(0, GDN_HEADS, 2)]
    wq = {}
    for (r, i) in chunks:
        for (a, b) in pairs:
            wq[r, i, a] = jnp.concatenate(
                [jnp.concatenate([w_h[r, i, a], w_h[r, i, b]], axis=1),
                 jnp.concatenate([qd[r, i, a], qd[r, i, b]], axis=1)],
                axis=0).astype(BF16)
    s_cur = {(r, h): s_scr[r, h] for r in range(rb) for h in range(GDN_HEADS)}
    for i in range(cps):
        rs, vn = {}, {}
        for r in range(rb):
            for (a, b) in pairs:
                rs[r, a] = _dot(wq[r, i, a], _block_diag([s_cur[r, a].astype(BF16), s_cur[r, b].astype(BF16)]))
        for r in range(rb):
            for (a, b) in pairs:
                vn[r, a] = (u_h[r, i, a] - rs[r, a][:L, :GDN_DV]).astype(BF16)
                vn[r, b] = (u_h[r, i, b] - rs[r, a][:L, GDN_DV:]).astype(BF16)
        for r in range(rb):
            for (a, b) in pairs:
                lo = (a % pk) * L
                o2 = _dot(qkg[r, i, a // pk][:, lo:lo + 2 * L], _block_diag([vn[r, a], vn[r, b]]))
                o_ref[r, i * L:(i + 1) * L, a * GDN_DV:(a + 2) * GDN_DV] = rs[r, a][L:, :] + o2
        for r in range(rb):
            for h in range(GDN_HEADS):
                s_cur[r, h] = dl[r, i, h] * s_cur[r, h] + _dot_tn(kdec[r, i, h], vn[r, h])

    for r in range(rb):
        for h in range(GDN_HEADS):
            s_scr[r, h] = s_cur[r, h]

    @pl.when(c == pl.num_programs(1) - 1)
    def _():
        sout_ref[...] = s_scr[...]


def _gdn(conv_act, ab, s0, alog_pad, dtb_pad, *, rb, cps):
    B, T, cd = conv_act.shape
    tg = cps * CHUNK
    state_spec = pl.BlockSpec((rb,) + s0.shape[1:], lambda b, c: (b, 0, 0, 0))
    return pl.pallas_call(
        functools.partial(_gdn_kernel, rb=rb, cps=cps),
        out_shape=[jax.ShapeDtypeStruct((B, T, GDN_V_DIM), F32),
                   jax.ShapeDtypeStruct(s0.shape, F32)],
        grid=(B // rb, T // tg),
        in_specs=[pl.BlockSpec((rb, tg, cd), lambda b, c: (b, c, 0)),
                  pl.BlockSpec((rb, tg, _AB_WIDTH), lambda b, c: (b, c, 0)),
                  state_spec, _const_spec((1, LANES)), _const_spec((1, LANES))],
        out_specs=[pl.BlockSpec((rb, tg, GDN_V_DIM), lambda b, c: (b, c, 0)), state_spec],
        scratch_shapes=[pltpu.VMEM((rb,) + s0.shape[1:], F32)],
        compiler_params=_params(("parallel", "arbitrary")),
        name="gdn",
    )(conv_act, ab, s0, alog_pad, dtb_pad)


def _segment_rms(x, gain, ones2):
    hi, lo = _split_bf16(x * x)
    ss = _dot(jnp.concatenate([hi, lo], axis=1), ones2)
    return x * lax.rsqrt(ss * (1.0 / SWA_HD) + EPS) * gain


def _swa_kernel(q_ref, kv_ref, kc_ref, vc_ref, qg_ref, kg_ref, sink_ref,
                o_ref, kout_ref, vout_ref, kwin, vwin, *, rb, tq, mask_history):
    c = pl.program_id(1)
    L = CHUNK
    span = WINDOW + L
    kvd = SWA_KV_DIM
    nj = tq // L

    @pl.when(c == 0)
    def _():
        kwin[:, 0:WINDOW, :] = kc_ref[...]
        vwin[:, 0:WINDOW, :] = vc_ref[...]

    seg_r = lax.broadcasted_iota(jnp.int32, (kvd, kvd), 0) // SWA_HD
    seg_c = lax.broadcasted_iota(jnp.int32, (kvd, kvd), 1) // SWA_HD
    ones_bd = jnp.where(seg_r == seg_c, 1.0, 0.0).astype(BF16)
    ones2 = jnp.concatenate([ones_bd, ones_bd], axis=0)

    rows = SWA_GROUP * L
    scale = SWA_HD ** -0.5
    qgain = qg_ref[...] * scale
    sink_all = jnp.concatenate([jnp.broadcast_to(sink_ref[:, h:h + 1], (L, LANES)) for h in range(SWA_HQ)],
                               axis=0)
    seg_o = lax.broadcasted_iota(jnp.int32, (rows, kvd), 1) // SWA_HD
    seg_w = lax.broadcasted_iota(jnp.int32, (WINDOW + tq, kvd), 1) // SWA_HD

    for r in range(rb):
        kv = kv_ref[r]
        kwin[r, WINDOW:WINDOW + tq, :] = _segment_rms(kv[:, :kvd], kg_ref[...], ones2)
        vwin[r, WINDOW:WINDOW + tq, :] = kv[:, kvd:]

    k_only, v_only, qn = {}, {}, {}
    for r in range(rb):
        kw, vw = kwin[r], vwin[r]
        for hk in range(SWA_HKV):
            k_only[r, hk] = jnp.where(seg_w == hk, kw, 0.0).astype(BF16)
            v_only[r, hk] = jnp.where(seg_w == hk, vw, 0.0).astype(BF16)
        for j in range(nj):
            q_stack = jnp.concatenate([q_ref[r, j * L:(j + 1) * L, g * kvd:(g + 1) * kvd]
                                       for g in range(SWA_GROUP)], axis=0)
            qn[r, j] = _segment_rms(q_stack, qgain, ones2).astype(BF16)

    def scores(r, j):
        return [_dot_nt(qn[r, j], k_only[r, hk][j * L:j * L + span]) for hk in range(SWA_HKV)]

    steps = [(r, j) for r in range(rb) for j in range(nj)]
    s_next = scores(*steps[0])
    for n, (r, j) in enumerate(steps):
        s_all = jnp.concatenate(s_next, axis=0)
        if n + 1 < len(steps):
            s_next = scores(*steps[n + 1])
        if mask_history:
            kpos = c * tq + (j * L - WINDOW) + lax.broadcasted_iota(jnp.int32, s_all.shape, 1)
            s_all = jnp.where(kpos >= 0, s_all, -jnp.inf)
        m = jnp.maximum(jnp.max(s_all, axis=-1, keepdims=True), sink_all)
        p = jnp.exp(s_all - jnp.concatenate([m, m[:, :span - LANES]], axis=1))
        den = jnp.sum(p, axis=-1, keepdims=True) + jnp.exp(sink_all - m)
        pb = p.astype(BF16)
        acc = None
        den_full = None
        for hk in range(SWA_HKV):
            pv = _dot(pb[hk * rows:(hk + 1) * rows], v_only[r, hk][j * L:j * L + span])
            acc = pv if acc is None else acc + pv
            dh = jnp.concatenate([den[hk * rows:(hk + 1) * rows]] * (kvd // LANES), axis=1)
            den_full = dh if den_full is None else jnp.where(seg_o == hk, dh, den_full)
        o = acc / den_full
        for g in range(SWA_GROUP):
            o_ref[r, j * L:(j + 1) * L, g * kvd:(g + 1) * kvd] = o[g * L:(g + 1) * L]

    knew = kwin[:, tq:tq + WINDOW, :]
    vnew = vwin[:, tq:tq + WINDOW, :]
    kwin[:, 0:WINDOW, :] = knew
    vwin[:, 0:WINDOW, :] = vnew

    @pl.when(c == pl.num_programs(1) - 1)
    def _():
        kout_ref[...] = knew
        vout_ref[...] = vnew


def _swa(q, kv, k_cache, v_cache, q_gain, k_gain, sinks, *, rb, tq, mask_history):
    B, T, _ = q.shape
    tok = lambda w: pl.BlockSpec((rb, tq, w), lambda b, c: (b, c, 0))
    win = pl.BlockSpec((rb, WINDOW, SWA_KV_DIM), lambda b, c: (b, 0, 0))
    return pl.pallas_call(
        functools.partial(_swa_kernel, rb=rb, tq=tq, mask_history=mask_history),
        out_shape=[jax.ShapeDtypeStruct((B, T, SWA_Q_DIM), F32),
                   jax.ShapeDtypeStruct((B, WINDOW, SWA_KV_DIM), F32),
                   jax.ShapeDtypeStruct((B, WINDOW, SWA_KV_DIM), F32)],
        grid=(B // rb, T // tq),
        in_specs=[tok(SWA_Q_DIM), tok(2 * SWA_KV_DIM), win, win,
                  _const_spec((1, SWA_KV_DIM)), _const_spec((1, SWA_KV_DIM)), _const_spec((1, SWA_HQ))],
        out_specs=[tok(SWA_Q_DIM), win, win],
        scratch_shapes=[pltpu.VMEM((rb, WINDOW + tq, SWA_KV_DIM), F32),
                        pltpu.VMEM((rb, WINDOW + tq, SWA_KV_DIM), F32)],
        compiler_params=_params(("parallel", "arbitrary")),
        name="swa",
    )(q, kv, k_cache, v_cache, q_gain, k_gain, sinks)


_FFN_ROWS = 512
_PROJ_ROWS = 256
_GDN_CHUNK_ROWS = 8
_SWA_QUERIES = 512


def _tile_plan(batch, seq):
    def rows_by_time(target):
        tt = min(seq, target)
        return max(1, min(batch, target // tt)), tt
    bb, tt = rows_by_time(_FFN_ROWS)
    pbb, ptt = rows_by_time(_PROJ_ROWS)
    cps = min(seq // CHUNK, 4)
    rb = max(1, min(batch, _GDN_CHUNK_ROWS // cps))
    srb, tq = rows_by_time(_SWA_QUERIES)
    srb = min(srb, 4)
    assert seq % CHUNK == 0 and seq % (cps * CHUNK) == 0, seq
    assert all(seq % t == 0 for t in (tt, ptt, tq)) and all(batch % b == 0 for b in (bb, pbb, rb, srb))
    return dict(bb=bb, tt=tt, pbb=pbb, ptt=ptt, rb=rb, cps=cps, srb=srb, tq=tq)


def _layer(x, mod, conv_prefix, s0, k_cache, v_cache, wts, *, mask_history):
    bb, tt, pbb, ptt, rb, cps, srb, tq = (_tile_plan(x.shape[0], x.shape[1])[k] for k in
                                            ("bb", "tt", "pbb", "ptt", "rb", "cps", "srb", "tq"))
    (norm_ffn1, ffn1_in, ffn1_out, norm_mix, w_main, w_ab, conv_w, alog_pad, dtb_pad, gnorm,
     q_gain, k_gain, sinks, b_merge, w_o, norm_ffn2, ffn2_in, ffn2_out, tf) = wts
    B, T, d = x.shape
    x = _ffn(x, mod[0:3], norm_ffn1, ffn1_in, ffn1_out, bb=bb, tt=tt, tf=tf)
    conv_act, z, q_s, kv_s, gates, ab, new_conv = _inproj(x, mod[3:6], norm_mix, w_main, w_ab,
                                                          conv_prefix, conv_w, bb=pbb, tt=ptt)
    o_g, new_s = _gdn(conv_act, ab, s0, alog_pad, dtb_pad, rb=rb, cps=cps)
    o_s, new_k, new_v = _swa(q_s, kv_s, k_cache, v_cache, q_gain, k_gain, sinks,
                             rb=srb, tq=tq, mask_history=mask_history)
    x = _mix_ffn(x, o_g, z, gnorm, o_s, gates, b_merge, w_o, mod[3:6], mod[6:9], norm_ffn2, ffn2_in,
                 ffn2_out, bb=bb, tt=tt, tf=tf)
    new_k = new_k.reshape(B, WINDOW, SWA_HKV, SWA_HD)
    new_v = new_v.reshape(B, WINDOW, SWA_HKV, SWA_HD)
    return x, new_conv, new_s, new_k, new_v


def _group_major(w, axis):
    shape = w.shape
    split = shape[:axis] + (SWA_HKV, SWA_GROUP, SWA_HD) + shape[axis + 1:]
    return jnp.swapaxes(w.reshape(split), axis, axis + 1).reshape(shape)


def _pad_lanes(v):
    return jnp.pad(v.astype(F32), (0, LANES - v.shape[0])).reshape(1, LANES)


def kernel(x_prompt, x_sample, state_gdn_conv, state_gdn, cache_swa_k, cache_swa_v, c_prompt, c_sample, w_ada, b_ada, norm_ffn1, ffn1_w_in, ffn1_w_out, norm_mix, w_in, gdn_conv_w, gdn_a_log, gdn_dt_bias, gdn_norm, swa_q_norm, swa_k_norm, swa_sinks, b_merge, w_out, norm_ffn2, ffn2_w_in, ffn2_w_out):
    depth = w_ada.shape[0]
    bp, tp, d = x_prompt.shape
    bs, ts, _ = x_sample.shape
    tf = 256
    yp, ys = x_prompt, x_sample
    outs_p, outs_s = [], []
    for l in range(depth):
        mod = _adaln(jnp.concatenate([c_prompt, c_sample], axis=0), w_ada[l], b_ada[l])
        mod = mod.reshape(N_MOD, bp + bs, 1, d)
        wl = w_in[l]
        o_z = GDN_CONV_DIM + GDN_V_DIM
        o_a, o_b = o_z, o_z + GDN_HEADS
        o_q = o_b + GDN_HEADS
        o_k = o_q + SWA_Q_DIM
        o_g = o_k + 2 * SWA_KV_DIM
        o_gb = o_g + d
        w_main = jnp.concatenate(
            [wl[:, :o_z], _group_major(wl[:, o_q:o_k], 1), wl[:, o_k:o_gb], _group_major(wl[:, o_gb:], 1)],
            axis=1).astype(BF16)
        pad = jnp.zeros((d, LANES - GDN_HEADS), wl.dtype)
        w_ab = jnp.concatenate([wl[:, o_a:o_b], pad, wl[:, o_b:o_q], pad], axis=1).astype(BF16)
        bm = jnp.concatenate([b_merge[l][:d], _group_major(b_merge[l][d:], 0)]).reshape(1, 2 * d)
        w_o2 = jnp.concatenate([w_out[l], _group_major(w_out[l], 0)], axis=0).astype(BF16)
        wts = (norm_ffn1[l].reshape(1, d), ffn1_w_in[l].astype(BF16), ffn1_w_out[l].astype(BF16),
               norm_mix[l].reshape(1, d), w_main, w_ab, gdn_conv_w[l],
               _pad_lanes(gdn_a_log[l]), _pad_lanes(gdn_dt_bias[l]), gdn_norm[l].reshape(1, GDN_DV),
               jnp.tile(swa_q_norm[l], SWA_HKV).reshape(1, SWA_KV_DIM),
               jnp.tile(swa_k_norm[l], SWA_HKV).reshape(1, SWA_KV_DIM),
               swa_sinks[l].reshape(1, SWA_HQ).astype(F32), bm,
               w_o2, norm_ffn2[l].reshape(1, d),
               ffn2_w_in[l].astype(BF16), ffn2_w_out[l].astype(BF16), tf)
        zero_conv = jnp.zeros((bp, CONV_W - 1, GDN_CONV_DIM), F32)
        zero_s = jnp.zeros((bp, GDN_HEADS, GDN_DK, GDN_DV), F32)
        zero_kv = jnp.zeros((bp, WINDOW, SWA_KV_DIM), F32)
        yp, *rest_p = _layer(yp, mod[:, :bp], zero_conv, zero_s, zero_kv, zero_kv, wts, mask_history=True)
        ys, *rest_s = _layer(ys, mod[:, bp:], state_gdn_conv[l], state_gdn[l],
                             cache_swa_k[l].reshape(bs, WINDOW, SWA_KV_DIM),
                             cache_swa_v[l].reshape(bs, WINDOW, SWA_KV_DIM), wts, mask_history=False)
        outs_p.append(rest_p)
        outs_s.append(rest_s)
    stack = lambda outs, i: jnp.stack([o[i] for o in outs])
    return (yp, ys,
            stack(outs_p, 0), stack(outs_p, 1), stack(outs_p, 2), stack(outs_p, 3),
            stack(outs_s, 0), stack(outs_s, 1), stack(outs_s, 2), stack(outs_s, 3))
```

```python
import functools

import jax
import jax.numpy as jnp
from jax import lax
from jax.experimental import pallas as pl
from jax.experimental.pallas import tpu as pltpu

F32 = jnp.float32
BF16 = jnp.bfloat16

CHUNK = 64
GDN_HEADS = 8
GDN_DK = 128
GDN_DV = 128
GDN_QK_DIM = GDN_HEADS * GDN_DK
GDN_V_DIM = GDN_HEADS * GDN_DV
GDN_CONV_DIM = 2 * GDN_QK_DIM + GDN_V_DIM
CONV_W = 4
SWA_HQ = 16
SWA_HKV = 4
SWA_HD = 64
SWA_GROUP = SWA_HQ // SWA_HKV
SWA_Q_DIM = SWA_HQ * SWA_HD
SWA_KV_DIM = SWA_HKV * SWA_HD
WINDOW = 128
N_MOD = 9
EPS = 1e-6
LANES = 128
VMEM_LIMIT_BYTES = 60 * 1024 * 1024


def _dot(a, b):
    return jnp.dot(a, b, preferred_element_type=F32)


def _dot_nt(a, b):
    return lax.dot_general(a, b, (((1,), (1,)), ((), ())), preferred_element_type=F32)


def _dot_tn(a, b, precision=None):
    return lax.dot_general(a, b, (((0,), (0,)), ((), ())), preferred_element_type=F32,
                           precision=precision)


def _split_bf16(x):
    hi = x.astype(BF16)
    return hi, (x - hi.astype(F32)).astype(BF16)


def _silu(x):
    return x * jax.nn.sigmoid(x)


def _softplus(x):
    return jnp.maximum(x, 0.0) + jnp.log1p(jnp.exp(-jnp.abs(x)))


def _modulate(x, gain, shift, scale):
    y = x * lax.rsqrt(jnp.mean(x * x, axis=-1, keepdims=True) + EPS)
    return (y * gain) * (1 + scale) + shift


def _const_spec(shape):
    nd = len(shape)
    return pl.BlockSpec(shape, lambda *_: (0,) * nd, pipeline_mode=pl.Buffered(1))


def _params(semantics):
    return pltpu.CompilerParams(dimension_semantics=semantics, vmem_limit_bytes=VMEM_LIMIT_BYTES)


def _adaln_kernel(c_ref, w_ref, b_ref, o_ref):
    s = _silu(c_ref[...]).astype(BF16)
    o_ref[0] = _dot(s, w_ref[...].astype(BF16)) + b_ref[0]


def _adaln(c, w_ada, b_ada):
    nb, d = c.shape
    return pl.pallas_call(
        _adaln_kernel,
        out_shape=jax.ShapeDtypeStruct((N_MOD, nb, d), F32),
        grid=(N_MOD,),
        in_specs=[pl.BlockSpec((nb, d), lambda j: (0, 0)),
                  pl.BlockSpec((d, d), lambda j: (0, j)),
                  pl.BlockSpec((1, 1, d), lambda j: (j, 0, 0))],
        out_specs=pl.BlockSpec((1, nb, d), lambda j: (j, 0, 0)),
        compiler_params=_params(("parallel",)),
        name="adaln",
    )(c, w_ada, b_ada.reshape(N_MOD, 1, d))


def _ffn_body(x, mod_ref, gain_ref, win_ref, wout_ref, act_ref, tf):
    bb, tt, d = x.shape
    d_ff = wout_ref.shape[0]
    sh, sc, gt = mod_ref[0], mod_ref[1], mod_ref[2]
    h = _modulate(x, gain_ref[...], sh, sc).reshape(bb * tt, d).astype(BF16)
    for j in range(d_ff // tf):
        gate = _dot(h, win_ref[:, j * tf:(j + 1) * tf])
        up = _dot(h, win_ref[:, d_ff + j * tf:d_ff + (j + 1) * tf])
        act_ref[:, j * tf:(j + 1) * tf] = (_silu(gate) * up).astype(BF16)
    y = _dot(act_ref[...], wout_ref[...])
    return x + (0.5 * gt) * y.reshape(bb, tt, d)


def _ffn_kernel(x_ref, mod_ref, gain_ref, win_ref, wout_ref, o_ref, act_ref, *, tf):
    o_ref[...] = _ffn_body(x_ref[...], mod_ref, gain_ref, win_ref, wout_ref, act_ref, tf)


def _mix_ffn_kernel(x_ref, og_ref, z_ref, gnorm_ref, os_ref, gates_ref, bm_ref, wo_ref, gt2_ref,
                    mod_ref, gain_ref, win_ref, wout_ref, o_ref, act_ref, *, tf):
    x = x_ref[...]
    bb, tt, d = x.shape
    gnorm = gnorm_ref[...]
    og_heads = []
    for h in range(GDN_HEADS):
        o = og_ref[:, :, h * GDN_DV:(h + 1) * GDN_DV]
        on = o * lax.rsqrt(jnp.mean(o * o, axis=-1, keepdims=True) + EPS) * gnorm
        og_heads.append(on * _silu(z_ref[:, :, h * GDN_DV:(h + 1) * GDN_DV].astype(F32)))
    gl = jax.nn.sigmoid(gates_ref[...].astype(F32) + bm_ref[...])
    mix = gl * jnp.concatenate(og_heads + [os_ref[...]], axis=-1)
    y = _dot(mix.reshape(bb * tt, 2 * d).astype(BF16), wo_ref[...]).reshape(bb, tt, d)
    x1 = x + gt2_ref[0] * y
    o_ref[...] = _ffn_body(x1, mod_ref, gain_ref, win_ref, wout_ref, act_ref, tf)


def _tok_spec(bb, tt, width):
    return pl.BlockSpec((bb, tt, width), lambda b, t: (b, t, 0))


def _mod_spec(n, bb, d):
    return pl.BlockSpec((n, bb, 1, d), lambda b, t: (0, b, 0, 0))


def _ffn(x, mod3, gain, w_in_bf16, w_out, *, bb, tt, tf):
    B, T, d = x.shape
    d_ff = w_out.shape[0]
    assert d_ff % tf == 0 and w_in_bf16.shape == (d, 2 * d_ff), (w_in_bf16.shape, tf)
    return pl.pallas_call(
        functools.partial(_ffn_kernel, tf=tf),
        out_shape=jax.ShapeDtypeStruct(x.shape, F32),
        grid=(B // bb, T // tt),
        in_specs=[_tok_spec(bb, tt, d), _mod_spec(3, bb, d), _const_spec((1, d)),
                  _const_spec((d, 2 * d_ff)), _const_spec((d_ff, d))],
        out_specs=_tok_spec(bb, tt, d),
        scratch_shapes=[pltpu.VMEM((bb * tt, d_ff), BF16)],
        compiler_params=_params(("parallel", "parallel")),
        name="ffn",
    )(x, mod3, gain, w_in_bf16, w_out)


def _mix_ffn(x, og, z, gnorm, os_, gates, b_merge, w_o, mod_mix, mod3, gain, w_in_bf16, w_out,
             *, bb, tt, tf):
    B, T, d = x.shape
    d_ff = w_out.shape[0]
    assert d_ff % tf == 0 and w_in_bf16.shape == (d, 2 * d_ff), (w_in_bf16.shape, tf)
    return pl.pallas_call(
        functools.partial(_mix_ffn_kernel, tf=tf),
        out_shape=jax.ShapeDtypeStruct(x.shape, F32),
        grid=(B // bb, T // tt),
        in_specs=[_tok_spec(bb, tt, d), _tok_spec(bb, tt, d), _tok_spec(bb, tt, d),
                  _const_spec((1, GDN_DV)), _tok_spec(bb, tt, d),
                  _tok_spec(bb, tt, 2 * d), _const_spec((1, 2 * d)), _const_spec((2 * d, d)),
                  pl.BlockSpec((1, bb, 1, d), lambda b, t: (2, b, 0, 0)),
                  _mod_spec(3, bb, d), _const_spec((1, d)),
                  _const_spec((d, 2 * d_ff)), _const_spec((d_ff, d))],
        out_specs=_tok_spec(bb, tt, d),
        scratch_shapes=[pltpu.VMEM((bb * tt, d_ff), BF16)],
        compiler_params=_params(("parallel", "parallel")),
        name="mix_ffn",
    )(x, og, z, gnorm, os_, gates, b_merge, w_o, mod_mix, mod3, gain, w_in_bf16, w_out)


_PROJ_STEP = 512
_AB_WIDTH = 2 * LANES
_XP_PAD = 8


def _inproj_kernel(x_ref, mod_ref, gain_ref, w_ref, wab_ref, prefix_ref, convw_ref,
                   conv_ref, z_ref, q_ref, kv_ref, gates_ref, ab_ref, convout_ref, xp_scr):
    t_idx = pl.program_id(1)
    x = x_ref[...]
    bb, tt, d = x.shape
    hist = CONV_W - 1

    @pl.when(t_idx == 0)
    def _():
        xp_scr[:, _XP_PAD - hist:_XP_PAD, :] = prefix_ref[...]

    h = _modulate(x, gain_ref[...], mod_ref[0], mod_ref[1]).reshape(bb * tt, d).astype(BF16)
    cw = convw_ref[...]
    for lo in range(0, GDN_CONV_DIM, _PROJ_STEP):
        hi = lo + _PROJ_STEP
        pre = _dot(h, w_ref[:, lo:hi]).reshape(bb, tt, hi - lo)
        xp_scr[:, _XP_PAD:_XP_PAD + tt, lo:hi] = pre
        y = pre * cw[hist:hist + 1, lo:hi]
        for i in range(hist):
            y = y + xp_scr[:, _XP_PAD - hist + i:_XP_PAD - hist + i + tt, lo:hi] * cw[i:i + 1, lo:hi]
        conv_ref[:, :, lo:hi] = _silu(y)
    tail = xp_scr[:, tt:tt + _XP_PAD, :]
    xp_scr[:, 0:_XP_PAD, :] = tail

    col = GDN_CONV_DIM
    for ref in (z_ref, q_ref, kv_ref, gates_ref):
        width = ref.shape[-1]
        for lo in range(0, width, _PROJ_STEP):
            hi = min(lo + _PROJ_STEP, width)
            ref[:, :, lo:hi] = _dot(h, w_ref[:, col + lo:col + hi]).reshape(bb, tt, hi - lo).astype(ref.dtype)
        col += width
    ab_ref[...] = _dot(h, wab_ref[...]).reshape(bb, tt, _AB_WIDTH)

    @pl.when(t_idx == pl.num_programs(1) - 1)
    def _():
        convout_ref[...] = tail[:, _XP_PAD - hist:, :]


def _inproj(x, mod3, gain, w_main, w_ab, prefix, conv_w, *, bb, tt):
    B, T, d = x.shape
    widths = (GDN_CONV_DIM, GDN_V_DIM, SWA_Q_DIM, 2 * SWA_KV_DIM, 2 * d, _AB_WIDTH)
    hist_spec = pl.BlockSpec((bb, CONV_W - 1, GDN_CONV_DIM), lambda b, t: (b, 0, 0))
    return pl.pallas_call(
        _inproj_kernel,
        out_shape=[jax.ShapeDtypeStruct((B, T, w), BF16 if n in (1, 2, 4) else F32)
                   for n, w in enumerate(widths)]
        + [jax.ShapeDtypeStruct((B, CONV_W - 1, GDN_CONV_DIM), F32)],
        grid=(B // bb, T // tt),
        in_specs=[_tok_spec(bb, tt, d), _mod_spec(3, bb, d), _const_spec((1, d)),
                  _const_spec(w_main.shape), _const_spec(w_ab.shape), hist_spec,
                  _const_spec(conv_w.shape)],
        out_specs=[_tok_spec(bb, tt, w) for w in widths] + [hist_spec],
        scratch_shapes=[pltpu.VMEM((bb, _XP_PAD + tt, GDN_CONV_DIM), F32)],
        compiler_params=_params(("parallel", "arbitrary")),
        name="inproj",
    )(x, mod3, gain, w_main, w_ab, prefix, conv_w)


GDN_PACK = 4


def _pack_masks(size, pk):
    row = lax.broadcasted_iota(jnp.int32, (size, pk * size), 0)
    lane = lax.broadcasted_iota(jnp.int32, (size, pk * size), 1)
    col = lane % size
    merges = []
    s = 1
    while s < size:
        rb, cb = row // s, col // s
        merges.append((rb == cb + 1) & (rb % 2 == 1))
        s *= 2
    return dict(seg=lane // size, eye=row == col, incl=row >= col, strict=row > col, merges=merges)


def _select_seg(seg, parts):
    out = parts[-1]
    for k in range(len(parts) - 2, -1, -1):
        out = jnp.where(seg == k, parts[k], out)
    return out


def _block_diag_packed(m, seg, pk):
    zero = jnp.zeros_like(m)
    return jnp.concatenate([jnp.where(seg == k, m, zero) for k in range(pk)], axis=0)


def _block_diag(blocks):
    n = len(blocks)
    zero = jnp.zeros_like(blocks[0])
    return jnp.concatenate(
        [jnp.concatenate([blocks[k] if j == k else zero for j in range(n)], axis=1) for k in range(n)],
        axis=0)


def _gdn_kernel(conv_ref, ab_ref, s0_ref, alog_ref, dtb_ref, o_ref, sout_ref, s_scr, *, rb, cps):
    c = pl.program_id(1)
    L = CHUNK
    pk = GDN_PACK
    ngrp = GDN_HEADS // pk

    @pl.when(c == 0)
    def _():
        s_scr[...] = s0_ref[...]

    mk = _pack_masks(L, pk)
    seg, incl, strict, merges = mk["seg"], mk["incl"], mk["strict"], mk["merges"]
    row64 = lax.broadcasted_iota(jnp.int32, (L, L), 0)
    col64 = lax.broadcasted_iota(jnp.int32, (L, L), 1)
    incl64 = jnp.where(row64 >= col64, 1.0, 0.0).astype(BF16)
    incl3 = jnp.concatenate([incl64] * 3, axis=1)
    upper_p = jnp.where(strict, 0.0, 1.0).astype(BF16)
    upper3 = jnp.concatenate([upper_p] * 3, axis=0)

    chunks = [(r, i) for r in range(rb) for i in range(cps)]
    units = [(r, i, p) for (r, i) in chunks for p in range(ngrp)]
    heads = [(r, i, h) for (r, i) in chunks for h in range(GDN_HEADS)]

    def head_cols(off, r, i, h):
        return conv_ref[r, i * L:(i + 1) * L, off + h * GDN_DK:off + (h + 1) * GDN_DK]

    beta, gc, g2 = {}, {}, {}
    for (r, i) in chunks:
        ab = ab_ref[r, i * L:(i + 1) * L, :]
        g = -jnp.exp(alog_ref[...]) * _softplus(ab[:, :LANES] + dtb_ref[...])
        beta[r, i] = jax.nn.sigmoid(ab[:, LANES:])
        g_hi = g.astype(BF16)
        g_mid, g_lo = _split_bf16(g - g_hi.astype(F32))
        g3 = jnp.concatenate([g_hi, g_mid, g_lo], axis=0)
        gc[r, i] = _dot(incl3, g3)
        g2[r, i] = _dot_tn(g3, upper3)

    qn, kn, vb, qd, kdec, dl, gcb, btb = {}, {}, {}, {}, {}, {}, {}, {}
    for (r, i, h) in heads:
        qh, kh = head_cols(0, r, i, h), head_cols(GDN_QK_DIM, r, i, h)
        qn[r, i, h] = qh * lax.rsqrt(jnp.sum(qh * qh, axis=-1, keepdims=True) + EPS) * (GDN_DK ** -0.5)
        kn[r, i, h] = kh * lax.rsqrt(jnp.sum(kh * kh, axis=-1, keepdims=True) + EPS)
        gcb[r, i, h] = jnp.broadcast_to(gc[r, i][:, h:h + 1], (L, LANES))
        btb[r, i, h] = jnp.broadcast_to(beta[r, i][:, h:h + 1], (L, LANES))
        egb = jnp.exp(gcb[r, i, h])
        glb = gcb[r, i, h][L - 1:L, :]
        vb[r, i, h] = jnp.concatenate([head_cols(2 * GDN_QK_DIM, r, i, h) * btb[r, i, h],
                                       kn[r, i, h] * (btb[r, i, h] * egb)], axis=-1).astype(BF16)
        qd[r, i, h] = qn[r, i, h] * egb
        kdec[r, i, h] = (kn[r, i, h] * jnp.exp(glb - gcb[r, i, h])).astype(BF16)
        dl[r, i, h] = jnp.exp(glb)

    def packed(cols):
        reps = (pk * L) // LANES
        return _select_seg(seg, [jnp.concatenate([cc] * reps, axis=1) for cc in cols])

    gamma, n_p, qkg = {}, {}, {}
    for (r, i, p) in units:
        hs = range(pk * p, pk * (p + 1))
        gr_p = _select_seg(seg[0:1], [g2[r, i][h:h + 1, :] for h in hs])
        gamma[r, i, p] = jnp.exp(jnp.where(incl, packed([gcb[r, i, h] for h in hs]) - gr_p, -jnp.inf))
    for (r, i, p) in units:
        hs = range(pk * p, pk * (p + 1))
        kb = [kn[r, i, h].astype(BF16) for h in hs]
        lhs = jnp.concatenate([jnp.concatenate(kb, axis=1),
                               jnp.concatenate([qn[r, i, h].astype(BF16) for h in hs], axis=1)],
                              axis=0)
        res = _dot_nt(lhs, _block_diag(kb))
        bt_p = packed([btb[r, i, h] for h in hs])
        n_p[r, i, p] = jnp.where(strict, bt_p * res[:L] * gamma[r, i, p], 0.0)
        qkg[r, i, p] = (res[L:] * gamma[r, i, p]).astype(BF16)

    t = {u: jnp.where(mk["eye"], 1.0, 0.0) - jnp.where(merges[0], n_p[u], 0.0) for u in units}
    for m in merges[1:]:
        ct = {u: _dot(jnp.where(m, n_p[u], 0.0).astype(BF16), _block_diag_packed(t[u].astype(BF16), seg, pk))
              for u in units}
        t = {u: t[u] - _dot(t[u].astype(BF16), _block_diag_packed(ct[u].astype(BF16), seg, pk))
             for u in units}

    u_h, w_h = {}, {}
    for (r, i, p) in units:
        hs = range(pk * p, pk * (p + 1))
        uw = _dot(t[r, i, p].astype(BF16), _block_diag([vb[r, i, h] for h in hs]))
        for k, h in enumerate(hs):
            u_h[r, i, h] = uw[:, 2 * k * GDN_DV:(2 * k + 1) * GDN_DV]
            w_h[r, i, h] = uw[:, (2 * k + 1) * GDN_DV:(2 * k + 2) * GDN_DV]

    pairs = [(a, a + 1) for a in range(0, GDN_HEADS, 2)]
    wq = {}
    for (r, i) in chunks:
        for (a, b) in pairs:
            wq[r, i, a] = jnp.concatenate(
                [jnp.concatenate([w_h[r, i, a], w_h[r, i, b]], axis=1),
                 jnp.concatenate([qd[r, i, a], qd[r, i, b]], axis=1)],
                axis=0).astype(BF16)
    s_cur = {(r, h): s_scr[r, h] for r in range(rb) for h in range(GDN_HEADS)}
    for i in range(cps):
        rs, vn = {}, {}
        for r in range(rb):
            for (a, b) in pairs:
                rs[r, a] = _dot(wq[r, i, a], _block_diag([s_cur[r, a].astype(BF16), s_cur[r, b].astype(BF16)]))
        for r in range(rb):
            for (a, b) in pairs:
                vn[r, a] = (u_h[r, i, a] - rs[r, a][:L, :GDN_DV]).astype(BF16)
                vn[r, b] = (u_h[r, i, b] - rs[r, a][:L, GDN_DV:]).astype(BF16)
        for r in range(rb):
            for (a, b) in pairs:
                lo = (a % pk) * L
                o2 = _dot(qkg[r, i, a // pk][:, lo:lo + 2 * L], _block_diag([vn[r, a], vn[r, b]]))
                o_ref[r, i * L:(i + 1) * L, a * GDN_DV:(a + 2) * GDN_DV] = rs[r, a][L:, :] + o2
        for r in range(rb):
            for h in range(GDN_HEADS):
                s_cur[r, h] = dl[r, i, h] * s_cur[r, h] + _dot_tn(kdec[r, i, h], vn[r, h])

    for r in range(rb):
        for h in range(GDN_HEADS):
            s_scr[r, h] = s_cur[r, h]

    @pl.when(c == pl.num_programs(1) - 1)
    def _():
        sout_ref[...] = s_scr[...]


def _gdn(conv_act, ab, s0, alog_pad, dtb_pad, *, rb, cps):
    B, T, cd = conv_act.shape
    tg = cps * CHUNK
    state_spec = pl.BlockSpec((rb,) + s0.shape[1:], lambda b, c: (b, 0, 0, 0))
    return pl.pallas_call(
        functools.partial(_gdn_kernel, rb=rb, cps=cps),
        out_shape=[jax.ShapeDtypeStruct((B, T, GDN_V_DIM), F32),
                   jax.ShapeDtypeStruct(s0.shape, F32)],
        grid=(B // rb, T // tg),
        in_specs=[pl.BlockSpec((rb, tg, cd), lambda b, c: (b, c, 0)),
                  pl.BlockSpec((rb, tg, _AB_WIDTH), lambda b, c: (b, c, 0)),
                  state_spec, _const_spec((1, LANES)), _const_spec((1, LANES))],
        out_specs=[pl.BlockSpec((rb, tg, GDN_V_DIM), lambda b, c: (b, c, 0)), state_spec],
        scratch_shapes=[pltpu.VMEM((rb,) + s0.shape[1:], F32)],
        compiler_params=_params(("parallel", "arbitrary")),
        name="gdn",
    )(conv_act, ab, s0, alog_pad, dtb_pad)


def _segment_rms(x, gain, ones2):
    hi, lo = _split_bf16(x * x)
    ss = _dot(jnp.concatenate([hi, lo], axis=1), ones2)
    return x * lax.rsqrt(ss * (1.0 / SWA_HD) + EPS) * gain


def _swa_kernel(q_ref, kv_ref, kc_ref, vc_ref, qg_ref, kg_ref, sink_ref,
                o_ref, kout_ref, vout_ref, kwin, vwin, *, rb, tq, mask_history):
    c = pl.program_id(1)
    L = CHUNK
    span = WINDOW + L
    kvd = SWA_KV_DIM
    nj = tq // L

    @pl.when(c == 0)
    def _():
        kwin[:, 0:WINDOW, :] = kc_ref[...]
        vwin[:, 0:WINDOW, :] = vc_ref[...]

    seg_r = lax.broadcasted_iota(jnp.int32, (kvd, kvd), 0) // SWA_HD
    seg_c = lax.broadcasted_iota(jnp.int32, (kvd, kvd), 1) // SWA_HD
    ones_bd = jnp.where(seg_r == seg_c, 1.0, 0.0).astype(BF16)
    ones2 = jnp.concatenate([ones_bd, ones_bd], axis=0)

    rows = SWA_GROUP * L
    scale = SWA_HD ** -0.5
    qgain = qg_ref[...] * scale
    sink_all = jnp.concatenate([jnp.broadcast_to(sink_ref[:, h:h + 1], (L, LANES)) for h in range(SWA_HQ)],
                               axis=0)
    seg_o = lax.broadcasted_iota(jnp.int32, (rows, kvd), 1) // SWA_HD
    seg_w = lax.broadcasted_iota(jnp.int32, (WINDOW + tq, kvd), 1) // SWA_HD

    for r in range(rb):
        kv = kv_ref[r]
        kwin[r, WINDOW:WINDOW + tq, :] = _segment_rms(kv[:, :kvd], kg_ref[...], ones2)
        vwin[r, WINDOW:WINDOW + tq, :] = kv[:, kvd:]

    k_only, v_only, qn = {}, {}, {}
    for r in range(rb):
        kw, vw = kwin[r], vwin[r]
        for hk in range(SWA_HKV):
            k_only[r, hk] = jnp.where(seg_w == hk, kw, 0.0).astype(BF16)
            v_only[r, hk] = jnp.where(seg_w == hk, vw, 0.0).astype(BF16)
        for j in range(nj):
            q_stack = jnp.concatenate([q_ref[r, j * L:(j + 1) * L, g * kvd:(g + 1) * kvd].astype(F32)
                                       for g in range(SWA_GROUP)], axis=0)
            qn[r, j] = _segment_rms(q_stack, qgain, ones2).astype(BF16)

    def scores(r, j):
        return [_dot_nt(qn[r, j], k_only[r, hk][j * L:j * L + span]) for hk in range(SWA_HKV)]

    steps = [(r, j) for r in range(rb) for j in range(nj)]
    s_next = scores(*steps[0])
    for n, (r, j) in enumerate(steps):
        s_all = jnp.concatenate(s_next, axis=0)
        if n + 1 < len(steps):
            s_next = scores(*steps[n + 1])
        if mask_history:
            kpos = c * tq + (j * L - WINDOW) + lax.broadcasted_iota(jnp.int32, s_all.shape, 1)
            s_all = jnp.where(kpos >= 0, s_all, -jnp.inf)
        m = jnp.maximum(jnp.max(s_all, axis=-1, keepdims=True), sink_all)
        p = jnp.exp(s_all - jnp.concatenate([m, m[:, :span - LANES]], axis=1))
        den = jnp.sum(p, axis=-1, keepdims=True) + jnp.exp(sink_all - m)
        pb = p.astype(BF16)
        acc = None
        den_full = None
        for hk in range(SWA_HKV):
            pv = _dot(pb[hk * rows:(hk + 1) * rows], v_only[r, hk][j * L:j * L + span])
            acc = pv if acc is None else acc + pv
            dh = jnp.concatenate([den[hk * rows:(hk + 1) * rows]] * (kvd // LANES), axis=1)
            den_full = dh if den_full is None else jnp.where(seg_o == hk, dh, den_full)
        o = acc / den_full
        for g in range(SWA_GROUP):
            o_ref[r, j * L:(j + 1) * L, g * kvd:(g + 1) * kvd] = o[g * L:(g + 1) * L]

    knew = kwin[:, tq:tq + WINDOW, :]
    vnew = vwin[:, tq:tq + WINDOW, :]
    kwin[:, 0:WINDOW, :] = knew
    vwin[:, 0:WINDOW, :] = vnew

    @pl.when(c == pl.num_programs(1) - 1)
    def _():
        kout_ref[...] = knew
        vout_ref[...] = vnew


def _swa(q, kv, k_cache, v_cache, q_gain, k_gain, sinks, *, rb, tq, mask_history):
    B, T, _ = q.shape
    tok = lambda w: pl.BlockSpec((rb, tq, w), lambda b, c: (b, c, 0))
    win = pl.BlockSpec((rb, WINDOW, SWA_KV_DIM), lambda b, c: (b, 0, 0))
    return pl.pallas_call(
        functools.partial(_swa_kernel, rb=rb, tq=tq, mask_history=mask_history),
        out_shape=[jax.ShapeDtypeStruct((B, T, SWA_Q_DIM), F32),
                   jax.ShapeDtypeStruct((B, WINDOW, SWA_KV_DIM), F32),
                   jax.ShapeDtypeStruct((B, WINDOW, SWA_KV_DIM), F32)],
        grid=(B // rb, T // tq),
        in_specs=[tok(SWA_Q_DIM), tok(2 * SWA_KV_DIM), win, win,
                  _const_spec((1, SWA_KV_DIM)), _const_spec((1, SWA_KV_DIM)), _const_spec((1, SWA_HQ))],
        out_specs=[tok(SWA_Q_DIM), win, win],
        scratch_shapes=[pltpu.VMEM((rb, WINDOW + tq, SWA_KV_DIM), F32),
                        pltpu.VMEM((rb, WINDOW + tq, SWA_KV_DIM), F32)],
        compiler_params=_params(("parallel", "arbitrary")),
        name="swa",
    )(q, kv, k_cache, v_cache, q_gain, k_gain, sinks)


_FFN_ROWS = 512
_PROJ_ROWS = 256
_GDN_CHUNK_ROWS = 8
_SWA_QUERIES = 512


def _tile_plan(batch, seq):
    def rows_by_time(target):
        tt = min(seq, target)
        return max(1, min(batch, target // tt)), tt
    bb, tt = rows_by_time(_FFN_ROWS)
    pbb, ptt = rows_by_time(_PROJ_ROWS)
    cps = min(seq // CHUNK, 4)
    rb = max(1, min(batch, _GDN_CHUNK_ROWS // cps))
    srb, tq = rows_by_time(_SWA_QUERIES)
    srb = min(srb, 4)
    assert seq % CHUNK == 0 and seq % (cps * CHUNK) == 0, seq
    assert all(seq % t == 0 for t in (tt, ptt, tq)) and all(batch % b == 0 for b in (bb, pbb, rb, srb))
    return dict(bb=bb, tt=tt, pbb=pbb, ptt=ptt, rb=rb, cps=cps, srb=srb, tq=tq)


def _layer(x, mod, conv_prefix, s0, k_cache, v_cache, wts, *, mask_history):
    bb, tt, pbb, ptt, rb, cps, srb, tq = (_tile_plan(x.shape[0], x.shape[1])[k] for k in
                                            ("bb", "tt", "pbb", "ptt", "rb", "cps", "srb", "tq"))
    (norm_ffn1, ffn1_in, ffn1_out, norm_mix, w_main, w_ab, conv_w, alog_pad, dtb_pad, gnorm,
     q_gain, k_gain, sinks, b_merge, w_o, norm_ffn2, ffn2_in, ffn2_out, tf) = wts
    B, T, d = x.shape
    x = _ffn(x, mod[0:3], norm_ffn1, ffn1_in, ffn1_out, bb=bb, tt=tt, tf=tf)
    conv_act, z, q_s, kv_s, gates, ab, new_conv = _inproj(x, mod[3:6], norm_mix, w_main, w_ab,
                                                          conv_prefix, conv_w, bb=pbb, tt=ptt)
    o_g, new_s = _gdn(conv_act, ab, s0, alog_pad, dtb_pad, rb=rb, cps=cps)
    o_s, new_k, new_v = _swa(q_s, kv_s, k_cache, v_cache, q_gain, k_gain, sinks,
                             rb=srb, tq=tq, mask_history=mask_history)
    x = _mix_ffn(x, o_g, z, gnorm, o_s, gates, b_merge, w_o, mod[3:6], mod[6:9], norm_ffn2, ffn2_in,
                 ffn2_out, bb=bb, tt=tt, tf=tf)
    new_k = new_k.reshape(B, WINDOW, SWA_HKV, SWA_HD)
    new_v = new_v.reshape(B, WINDOW, SWA_HKV, SWA_HD)
    return x, new_conv, new_s, new_k, new_v


def _group_major(w, axis):
    shape = w.shape
    split = shape[:axis] + (SWA_HKV, SWA_GROUP, SWA_HD) + shape[axis + 1:]
    return jnp.swapaxes(w.reshape(split), axis, axis + 1).reshape(shape)


def _pad_lanes(v):
    return jnp.pad(v.astype(F32), (0, LANES - v.shape[0])).reshape(1, LANES)


def kernel(x_prompt, x_sample, state_gdn_conv, state_gdn, cache_swa_k, cache_swa_v, c_prompt, c_sample, w_ada, b_ada, norm_ffn1, ffn1_w_in, ffn1_w_out, norm_mix, w_in, gdn_conv_w, gdn_a_log, gdn_dt_bias, gdn_norm, swa_q_norm, swa_k_norm, swa_sinks, b_merge, w_out, norm_ffn2, ffn2_w_in, ffn2_w_out):
    depth = w_ada.shape[0]
    bp, tp, d = x_prompt.shape
    bs, ts, _ = x_sample.shape
    tf = 256
    yp, ys = x_prompt, x_sample
    outs_p, outs_s = [], []
    for l in range(depth):
        mod = _adaln(jnp.concatenate([c_prompt, c_sample], axis=0), w_ada[l], b_ada[l])
        mod = mod.reshape(N_MOD, bp + bs, 1, d)
        wl = w_in[l]
        o_z = GDN_CONV_DIM + GDN_V_DIM
        o_a, o_b = o_z, o_z + GDN_HEADS
        o_q = o_b + GDN_HEADS
        o_k = o_q + SWA_Q_DIM
        o_g = o_k + 2 * SWA_KV_DIM
        o_gb = o_g + d
        w_main = jnp.concatenate(
            [wl[:, :o_z], _group_major(wl[:, o_q:o_k], 1), wl[:, o_k:o_gb], _group_major(wl[:, o_gb:], 1)],
            axis=1).astype(BF16)
        pad = jnp.zeros((d, LANES - GDN_HEADS), wl.dtype)
        w_ab = jnp.concatenate([wl[:, o_a:o_b], pad, wl[:, o_b:o_q], pad], axis=1).astype(BF16)
        bm = jnp.concatenate([b_merge[l][:d], _group_major(b_merge[l][d:], 0)]).reshape(1, 2 * d)
        w_o2 = jnp.concatenate([w_out[l], _group_major(w_out[l], 0)], axis=0).astype(BF16)
        wts = (norm_ffn1[l].reshape(1, d), ffn1_w_in[l].astype(BF16), ffn1_w_out[l].astype(BF16),
               norm_mix[l].reshape(1, d), w_main, w_ab, gdn_conv_w[l],
               _pad_lanes(gdn_a_log[l]), _pad_lanes(gdn_dt_bias[l]), gdn_norm[l].reshape(1, GDN_DV),
               jnp.tile(swa_q_norm[l], SWA_HKV).reshape(1, SWA_KV_DIM),
               jnp.tile(swa_k_norm[l], SWA_HKV).reshape(1, SWA_KV_DIM),
               swa_sinks[l].reshape(1, SWA_HQ).astype(F32), bm,
               w_o2, norm_ffn2[l].reshape(1, d),
               ffn2_w_in[l].astype(BF16), ffn2_w_out[l].astype(BF16), tf)
        zero_conv = jnp.zeros((bp, CONV_W - 1, GDN_CONV_DIM), F32)
        zero_s = jnp.zeros((bp, GDN_HEADS, GDN_DK, GDN_DV), F32)
        zero_kv = jnp.zeros((bp, WINDOW, SWA_KV_DIM), F32)
        yp, *rest_p = _layer(yp, mod[:, :bp], zero_conv, zero_s, zero_kv, zero_kv, wts, mask_history=True)
        ys, *rest_s = _layer(ys, mod[:, bp:], state_gdn_conv[l], state_gdn[l],
                             cache_swa_k[l].reshape(bs, WINDOW, SWA_KV_DIM),
                             cache_swa_v[l].reshape(bs, WINDOW, SWA_KV_DIM), wts, mask_history=False)
        outs_p.append(rest_p)
        outs_s.append(rest_s)
    stack = lambda outs, i: jnp.stack([o[i] for o in outs])
    return (yp, ys,
            stack(outs_p, 0), stack(outs_p, 1), stack(outs_p, 2), stack(outs_p, 3),
            stack(outs_s, 0), stack(outs_s, 1), stack(outs_s, 2), stack(outs_s, 3))
```
